```python
import jax, jax.numpy as jnp
from jax import lax
import numpy as np

D_MODEL = 1024
BATCH = 8
SEQ = 2048
DEPTH = 4
DEC_BATCH = 8
DEC_SEQ = 64
PAST_LEN = 2048

CHUNK = 64
Q_BLOCK = 128
N_EVEN = (DEPTH + 1) // 2
N_ODD = DEPTH // 2
WA = D_MODEL // 2
HA = 8
BWA = WA // HA
CONV_W = 4
LRU_C = 8.0
HB = 8
DHB = (D_MODEL // 2) // HB
WB = HB * DHB
AB_IN = 2 * WA + 3 * WB
AB_OUT = WA + WB
HC = D_MODEL // 256
DKC = D_MODEL // HC
DVC = 2 * DKC
WC_V = HC * DVC
C_IN = 2 * D_MODEL + 2 * WC_V
ROPE_BASE = 10000.0
D_FF = 2816
P_DIM = 256
N_NORMS = 8
EPS = 1e-6
F32 = jnp.float32

kernel_name = 'hybrid_lru_stickbreak_retention_stream_step'


def rms_norm(x, g):
    xf = x.astype(F32)
    y = xf * lax.rsqrt(jnp.mean(xf * xf, axis=-1, keepdims=True) + EPS)
    return (y * g.astype(F32)).astype(x.dtype)


def swiglu(x, w_gate, w_up, w_down):
    return (jax.nn.silu(x @ w_gate) * (x @ w_up)) @ w_down


def causal_conv(x, buf, w, b):
    L = x.shape[1]
    xp = jnp.concatenate([buf.astype(x.dtype), x], axis=1)
    y = b
    for tap in range(CONV_W):
        y = y + xp[:, tap:tap + L] * w[tap]
    return y, xp[:, -(CONV_W - 1):]


def rg_lru(xc, h0, lam, w_r, b_r, w_i, b_i):
    B, L, _ = xc.shape
    xf = xc.astype(F32)
    xh = xf.reshape(B, L, HA, BWA)
    rg = jax.nn.sigmoid(jnp.einsum('blhi,hij->blhj', xh, w_r.astype(F32)).reshape(B, L, WA) + b_r.astype(F32))
    ig = jax.nn.sigmoid(jnp.einsum('blhi,hij->blhj', xh, w_i.astype(F32)).reshape(B, L, WA) + b_i.astype(F32))
    log_a = -LRU_C * rg * jax.nn.softplus(-lam.astype(F32))
    a = jnp.exp(log_a)
    u = jnp.sqrt(-jnp.expm1(2.0 * log_a)) * ig * xf
    u = u.at[:, 0].add(a[:, 0] * h0.astype(F32))

    def combine(left, right):
        a1, b1 = left
        a2, b2 = right
        return a1 * a2, a2 * b1 + b2

    _, h = lax.associative_scan(combine, (a, u), axis=1)
    return h, h[:, -1]


def stick_breaking_block(q, k, v, q_pos0):
    z = jnp.einsum('bqhd,bkhd->bhqk', q.astype(F32), k.astype(F32)) * (DHB ** -0.5)
    t = q_pos0 + jnp.arange(q.shape[1])
    s = jnp.arange(k.shape[1])
    mask = (s[None, :] < t[:, None])[None, None]
    log_1mb = jnp.where(mask, jax.nn.log_sigmoid(-z), 0.0)
    tail = lax.cumsum(log_1mb, axis=3, reverse=True) - log_1mb
    att = jnp.where(mask, jnp.exp(jax.nn.log_sigmoid(z) + tail), 0.0)
    return jnp.einsum('bhqk,bkhd->bqhd', att, v.astype(F32)).astype(q.dtype)


def stick_breaking(q, k_all, v_all):
    L = q.shape[1]
    P = k_all.shape[1] - L
    outs = []
    for s0 in range(0, L, Q_BLOCK):
        e = min(s0 + Q_BLOCK, L)
        outs.append(stick_breaking_block(q[:, s0:e], k_all[:, :P + e], v_all[:, :P + e], P + s0))
    return jnp.concatenate(outs, axis=1)


def mixer_ab(h, conv_buf, lru_h0, past_k, past_v, w_in, w_out, conv_w, conv_b, w_r, b_r, w_i, b_i, lam):
    B, L, _ = h.shape
    xa, ga, q, k, v = jnp.split(h @ w_in, [WA, 2 * WA, 2 * WA + WB, 2 * WA + 2 * WB], axis=-1)
    xc, new_buf = causal_conv(xa, conv_buf, conv_w, conv_b)
    hs, h_last = rg_lru(xc, lru_h0, lam, w_r, b_r, w_i, b_i)
    out_a = jax.nn.gelu(ga) * hs.astype(h.dtype)
    q = q.reshape(B, L, HB, DHB)
    k = k.reshape(B, L, HB, DHB)
    v = v.reshape(B, L, HB, DHB)
    k_all = jnp.concatenate([past_k.astype(k.dtype), k], axis=1)
    v_all = jnp.concatenate([past_v.astype(v.dtype), v], axis=1)
    out_b = stick_breaking(q, k_all, v_all).reshape(B, L, WB)
    y = jnp.concatenate([out_a, out_b], axis=-1) @ w_out
    return y, h_last.astype(h.dtype), new_buf, k, v


def rotary(x, pos):
    half = x.shape[-1] // 2
    freq = ROPE_BASE ** (-jnp.arange(half, dtype=F32) / half)
    ang = pos.astype(F32)[:, None] * freq[None, :]
    cos = jnp.cos(ang)[None, :, None, :]
    sin = jnp.sin(ang)[None, :, None, :]
    x1, x2 = x[..., :half], x[..., half:]
    return jnp.concatenate([x1 * cos - x2 * sin, x2 * cos + x1 * sin], axis=-1)


def retention_chunk(q, k, v, R, log_g):
    L = q.shape[1]
    idx = jnp.arange(L, dtype=F32)
    dist = jnp.abs(idx[:, None] - idx[None, :])
    decay = jnp.exp(log_g[:, None, None] * dist)
    scores = jnp.einsum('blhd,bmhd->bhlm', q, k) * decay[None]
    o = jnp.einsum('bhlm,bmhe->blhe', scores, v)
    xi = jnp.exp(log_g[None, :] * (idx[:, None] + 1.0))
    o = o + jnp.einsum('blhd,bhde->blhe', q, R) * xi[None, :, :, None]
    zeta = jnp.exp(log_g[None, :] * (L - 1.0 - idx[:, None]))
    R = jnp.exp(log_g * L)[None, :, None, None] * R + jnp.einsum('blhd,blhe->bhde', k * zeta[None, :, :, None], v)
    return o, R


def mixer_c(h, R0, pos0, w_in, w_out):
    B, L, _ = h.shape
    q, k, v, g = jnp.split(h @ w_in, [D_MODEL, 2 * D_MODEL, 2 * D_MODEL + WC_V], axis=-1)
    pos = pos0 + jnp.arange(L)
    q = rotary(q.reshape(B, L, HC, DKC).astype(F32), pos)
    k = rotary(k.reshape(B, L, HC, DKC).astype(F32), pos) * (DKC ** -0.5)
    v = v.reshape(B, L, HC, DVC).astype(F32)
    log_g = jnp.log(1.0 - 2.0 ** (-5.0 - jnp.arange(HC, dtype=F32)))
    cl = min(L, CHUNK)
    n = L // cl

    def to_chunks(t):
        return t.reshape(B, n, cl, *t.shape[2:]).swapaxes(0, 1)

    def step(R, qkv):
        o, R = retention_chunk(qkv[0], qkv[1], qkv[2], R, log_g)
        return R, o

    R_last, o = lax.scan(step, R0.astype(F32), (to_chunks(q), to_chunks(k), to_chunks(v)))
    o = o.swapaxes(0, 1).reshape(B, L, HC, DVC)
    mu = jnp.mean(o, axis=-1, keepdims=True)
    var = jnp.mean(jnp.square(o - mu), axis=-1, keepdims=True)
    o = ((o - mu) * lax.rsqrt(var + EPS)).reshape(B, L, WC_V)
    y = (jax.nn.silu(g.astype(F32)) * o).astype(h.dtype) @ w_out
    return y, R_last.astype(h.dtype)


def run_group(x, p, conv_state, lru_state, past_k, past_v, ret_state,
              norm_g, ffn_w_gate, ffn_w_up, ffn_w_down, ple_w_in, ple_w_gate,
              ab_w_in, ab_w_out, lru_conv_w, lru_conv_b, lru_w_r, lru_b_r, lru_w_i, lru_b_i, lru_lambda,
              ret_w_in, ret_w_out):
    pos0 = past_k.shape[2]
    new_h, new_conv, new_k, new_v, new_ret = [], [], [], [], []
    for i in range(DEPTH):
        g = norm_g[i]
        j = i // 2
        x = x + 0.5 * rms_norm(swiglu(rms_norm(x, g[0]), ffn_w_gate[i, 0], ffn_w_up[i, 0], ffn_w_down[i, 0]), g[1])
        hn = rms_norm(x, g[2])
        if i % 2 == 0:
            mix, h_last, buf, k_rows, v_rows = mixer_ab(
                hn, conv_state[j], lru_state[j], past_k[j], past_v[j], ab_w_in[j], ab_w_out[j],
                lru_conv_w[j], lru_conv_b[j], lru_w_r[j], lru_b_r[j], lru_w_i[j], lru_b_i[j], lru_lambda[j])
            new_h.append(h_last)
            new_conv.append(buf)
            new_k.append(k_rows)
            new_v.append(v_rows)
        else:
            mix, R = mixer_c(hn, ret_state[j], pos0, ret_w_in[j], ret_w_out[j])
            new_ret.append(R)
        x = x + rms_norm(mix, g[3])
        x = x + 0.5 * rms_norm(swiglu(rms_norm(x, g[4]), ffn_w_gate[i, 1], ffn_w_up[i, 1], ffn_w_down[i, 1]), g[5])
        gate = jax.nn.sigmoid(rms_norm(x, g[6]) @ ple_w_gate[i])
        x = x + rms_norm(gate * (p[i] @ ple_w_in[i]), g[7])
    return x, jnp.stack(new_h), jnp.stack(new_conv), jnp.stack(new_k), jnp.stack(new_v), jnp.stack(new_ret)


def setup_inputs(seed: int = 0) -> dict:
    key = jax.random.key(seed)
    ks = jax.random.split(key, 32)
    nrm = jax.random.normal
    u = jax.random.uniform(ks[25], (N_EVEN, WA), F32, 0.9, 0.999)
    a = u ** (1.0 / LRU_C)
    lam = jnp.log(a) - jnp.log1p(-a)
    return {
        'x_prompt': nrm(ks[0], (BATCH, SEQ, D_MODEL), F32),
        'x_sample': nrm(ks[1], (DEC_BATCH, DEC_SEQ, D_MODEL), F32),
        'p_prompt': nrm(ks[2], (DEPTH, BATCH, SEQ, P_DIM), F32),
        'p_sample': nrm(ks[3], (DEPTH, DEC_BATCH, DEC_SEQ, P_DIM), F32),
        'state_lru_h': 0.5 * nrm(ks[4], (N_EVEN, DEC_BATCH, WA), F32),
        'state_conv': nrm(ks[5], (N_EVEN, DEC_BATCH, CONV_W - 1, WA), F32),
        'cache_sb_k': nrm(ks[6], (N_EVEN, DEC_BATCH, PAST_LEN, HB, DHB), F32),
        'cache_sb_v': nrm(ks[7], (N_EVEN, DEC_BATCH, PAST_LEN, HB, DHB), F32),
        'state_ret': 0.1 * nrm(ks[8], (N_ODD, DEC_BATCH, HC, DKC, DVC), F32),
        'norm_g': 1.0 + 0.05 * nrm(ks[9], (DEPTH, N_NORMS, D_MODEL), F32),
        'ffn_w_gate': nrm(ks[10], (DEPTH, 2, D_MODEL, D_FF), F32) * D_MODEL ** -0.5,
        'ffn_w_up': nrm(ks[11], (DEPTH, 2, D_MODEL, D_FF), F32) * D_MODEL ** -0.5,
        'ffn_w_down': nrm(ks[12], (DEPTH, 2, D_FF, D_MODEL), F32) * D_FF ** -0.5,
        'ple_w_in': nrm(ks[13], (DEPTH, P_DIM, D_MODEL), F32) * P_DIM ** -0.5,
        'ple_w_gate': nrm(ks[14], (DEPTH, D_MODEL, D_MODEL), F32) * D_MODEL ** -0.5,
        'ab_w_in': nrm(ks[15], (N_EVEN, D_MODEL, AB_IN), F32) * D_MODEL ** -0.5,
        'ab_w_out': nrm(ks[16], (N_EVEN, AB_OUT, D_MODEL), F32) * AB_OUT ** -0.5,
        'lru_conv_w': nrm(ks[17], (N_EVEN, CONV_W, WA), F32) * CONV_W ** -0.5,
        'lru_conv_b': 0.01 * nrm(ks[18], (N_EVEN, WA), F32),
        'lru_w_r': nrm(ks[19], (N_EVEN, HA, BWA, BWA), F32) * BWA ** -0.5,
        'lru_b_r': 0.01 * nrm(ks[20], (N_EVEN, WA), F32),
        'lru_w_i': nrm(ks[21], (N_EVEN, HA, BWA, BWA), F32) * BWA ** -0.5,
        'lru_b_i': 0.01 * nrm(ks[22], (N_EVEN, WA), F32),
        'lru_lambda': lam,
        'ret_w_in': nrm(ks[23], (N_ODD, D_MODEL, C_IN), F32) * D_MODEL ** -0.5,
        'ret_w_out': nrm(ks[24], (N_ODD, WC_V, D_MODEL), F32) * WC_V ** -0.5,
    }


def reference(x_prompt, x_sample, p_prompt, p_sample, state_lru_h, state_conv, cache_sb_k, cache_sb_v, state_ret,
              norm_g, ffn_w_gate, ffn_w_up, ffn_w_down, ple_w_in, ple_w_gate,
              ab_w_in, ab_w_out, lru_conv_w, lru_conv_b, lru_w_r, lru_b_r, lru_w_i, lru_b_i, lru_lambda,
              ret_w_in, ret_w_out):
    B = x_prompt.shape[0]
    dt = x_prompt.dtype
    conv0 = jnp.zeros((N_EVEN, B, CONV_W - 1, WA), dt)
    h0 = jnp.zeros((N_EVEN, B, WA), dt)
    kv0 = jnp.zeros((N_EVEN, B, 0, HB, DHB), dt)
    r0 = jnp.zeros((N_ODD, B, HC, DKC, DVC), dt)
    y_p, h_p, c_p, k_p, v_p, r_p = run_group(
        x_prompt, p_prompt, conv0, h0, kv0, kv0, r0,
        norm_g, ffn_w_gate, ffn_w_up, ffn_w_down, ple_w_in, ple_w_gate,
        ab_w_in, ab_w_out, lru_conv_w, lru_conv_b, lru_w_r, lru_b_r, lru_w_i, lru_b_i, lru_lambda,
        ret_w_in, ret_w_out)
    y_s, h_s, c_s, k_s, v_s, r_s = run_group(
        x_sample, p_sample, state_conv, state_lru_h, cache_sb_k, cache_sb_v, state_ret,
        norm_g, ffn_w_gate, ffn_w_up, ffn_w_down, ple_w_in, ple_w_gate,
        ab_w_in, ab_w_out, lru_conv_w, lru_conv_b, lru_w_r, lru_b_r, lru_w_i, lru_b_i, lru_lambda,
        ret_w_in, ret_w_out)
    return (y_p, y_s, h_p, c_p, k_p, v_p, r_p, h_s, c_s, k_s, v_s, r_s)
```

```python
import functools

import jax
import jax.numpy as jnp
from jax import lax
from jax.experimental import pallas as pl
from jax.experimental.pallas import tpu as pltpu

F32 = jnp.float32
BF16 = jnp.bfloat16

EPS = 1e-6
LRU_C = 8.0
CONV_W = 4
CHUNK_BITS = 6
ROPE_BASE = 10000.0
HEAD_B = 64
HEAD_PAIR = 2 * HEAD_B
SUBLANES = 8
FF_SUB = 256
VMEM_LIMIT_BYTES = 56 * 1024 * 1024


def _cparams(*sem):
    return pltpu.CompilerParams(dimension_semantics=sem, vmem_limit_bytes=VMEM_LIMIT_BYTES)


def _dot(a, b):
    return jnp.dot(a, b, preferred_element_type=F32)


def _dot_nt(a, b):
    return lax.dot_general(a, b, (((1,), (1,)), ((), ())), preferred_element_type=F32)


def _dot_tn(a, b):
    return lax.dot_general(a, b, (((0,), (0,)), ((), ())), preferred_element_type=F32)


def _rms(x, g):
    return x * lax.rsqrt(jnp.mean(x * x, axis=-1, keepdims=True) + EPS) * g


def _sigmoid(x):
    return 1.0 / (1.0 + jnp.exp(-x))


def _softplus(x):
    return jnp.maximum(x, 0.0) + jnp.log(1.0 + jnp.exp(-jnp.abs(x)))


def _gelu_tanh(x):
    return 0.5 * x * (1.0 + jnp.tanh(0.7978845608028654 * (x + 0.044715 * x * x * x)))


def _row_tile(n, pref):
    t = min(n, pref)
    while n % t:
        t //= 2
    return t


def _full(shape):
    return pl.BlockSpec(shape, lambda *_: (0,) * len(shape))


def _ffn_kernel(x_ref, gpre_ref, gpost_ref, wg_ref, wu_ref, wd_ref, o_ref, acc_ref, *, n_sub):
    xn = _rms(x_ref[...], gpre_ref[...]).astype(BF16)
    for c in range(n_sub):
        sl = pl.ds(c * FF_SUB, FF_SUB)
        h = _dot(xn, wg_ref[:, sl])
        u = _dot(xn, wu_ref[:, sl])
        a = (h * _sigmoid(h) * u).astype(BF16)
        d = _dot(a, wd_ref[sl, :])
        if c == 0:
            acc_ref[...] = d
        else:
            acc_ref[...] += d
    o_ref[...] = x_ref[...] + 0.5 * _rms(acc_ref[...], gpost_ref[...])


def _ffn(x, gpre, gpost, wg, wu, wd):
    n, d = x.shape
    f = wg.shape[1]
    tm = _row_tile(n, 1024)
    once = pl.Buffered(1)
    return pl.pallas_call(
        functools.partial(_ffn_kernel, n_sub=f // FF_SUB),
        grid=(n // tm,),
        in_specs=[
            pl.BlockSpec((tm, d), lambda i: (i, 0)),
            _full((1, d)),
            _full((1, d)),
            pl.BlockSpec((d, f), lambda i: (0, 0), pipeline_mode=once),
            pl.BlockSpec((d, f), lambda i: (0, 0), pipeline_mode=once),
            pl.BlockSpec((f, d), lambda i: (0, 0), pipeline_mode=once),
        ],
        out_specs=pl.BlockSpec((tm, d), lambda i: (i, 0)),
        out_shape=jax.ShapeDtypeStruct((n, d), F32),
        scratch_shapes=[pltpu.VMEM((tm, d), F32)],
        compiler_params=_cparams("parallel"),
        name="ffn",
    )(x, gpre, gpost, wg, wu, wd)


def _ple_kernel(x_ref, p_ref, gpre_ref, gpost_ref, wgate_ref, win_ref, o_ref):
    x = x_ref[...]
    gate = _sigmoid(_dot(_rms(x, gpre_ref[...]).astype(BF16), wgate_ref[...]))
    e = _dot(p_ref[...].astype(BF16), win_ref[...])
    o_ref[...] = x + _rms(gate * e, gpost_ref[...])


def _ple(x, p, gpre, gpost, wgate, win):
    n, d = x.shape
    pd = p.shape[1]
    tm = _row_tile(n, 1024)
    return pl.pallas_call(
        _ple_kernel,
        grid=(n // tm,),
        in_specs=[
            pl.BlockSpec((tm, d), lambda i: (i, 0)),
            pl.BlockSpec((tm, pd), lambda i: (i, 0)),
            _full((1, d)),
            _full((1, d)),
            _full((d, d)),
            _full((pd, d)),
        ],
        out_specs=pl.BlockSpec((tm, d), lambda i: (i, 0)),
        out_shape=jax.ShapeDtypeStruct((n, d), F32),
        compiler_params=_cparams("parallel"),
        name="ple",
    )(x, p, gpre, gpost, wgate, win)


def _ab_in_kernel(x_ref, g_ref, w_ref, xa_ref, ga_ref, q_ref, k_ref, v_ref, *, w):
    hn = _rms(x_ref[...], g_ref[...]).astype(BF16)
    for idx, o_ref in enumerate((xa_ref, ga_ref, q_ref, k_ref, v_ref)):
        o_ref[...] = _dot(hn, w_ref[:, idx * w:(idx + 1) * w])


def _ab_in(x, g, w_in):
    n, d = x.shape
    w = w_in.shape[1] // 5
    tm = _row_tile(n, 1024)
    out = jax.ShapeDtypeStruct((n, w), F32)
    ospec = pl.BlockSpec((tm, w), lambda i: (i, 0))
    return pl.pallas_call(
        functools.partial(_ab_in_kernel, w=w),
        grid=(n // tm,),
        in_specs=[pl.BlockSpec((tm, d), lambda i: (i, 0)), _full((1, d)), _full((d, 5 * w))],
        out_specs=[ospec] * 5,
        out_shape=[out] * 5,
        compiler_params=_cparams("parallel"),
        name="ab_in",
    )(x, g, w_in)


def _lru_kernel(xa_ref, ga_ref, buf_ref, h0_ref, cw_ref, cb_ref, wr_ref, br_ref, wi_ref, bi_ref, lam_ref,
                oa_ref, hl_ref, nb_ref, hc_ref, xc_ref, *, t):
    l = pl.program_id(1)

    @pl.when(l == 0)
    def _():
        hc_ref[...] = h0_ref[0]
        xc_ref[...] = jnp.zeros_like(xc_ref)
        xc_ref[SUBLANES - (CONV_W - 1):, :] = buf_ref[0]

    x = xa_ref[0]
    prev = xc_ref[...]
    row8 = lax.broadcasted_iota(jnp.int32, (SUBLANES, 1), 0)
    cw = cw_ref[...]
    y = cb_ref[...] + x * cw[CONV_W - 1:CONV_W]
    for k in range(1, CONV_W):
        xs = pltpu.roll(x, k, 0)
        head = jnp.where(row8 < k, pltpu.roll(prev, k, 0), xs[:SUBLANES])
        xs = jnp.concatenate([head, xs[SUBLANES:]], axis=0)
        y = y + xs * cw[CONV_W - 1 - k:CONV_W - k]
    xc_ref[...] = x[t - SUBLANES:]

    yb = y.astype(BF16)
    rg = _sigmoid(_dot(yb, wr_ref[...]) + br_ref[...])
    ig = _sigmoid(_dot(yb, wi_ref[...]) + bi_ref[...])
    log_a = (-LRU_C) * rg * _softplus(-lam_ref[...])
    a = jnp.exp(log_a)
    b = jnp.sqrt(1.0 - jnp.exp(2.0 * log_a)) * ig * y

    row = lax.broadcasted_iota(jnp.int32, (t, 1), 0)
    d = 1
    while d < t:
        keep = row >= d
        a_sh = jnp.where(keep, pltpu.roll(a, d, 0), 1.0)
        b_sh = jnp.where(keep, pltpu.roll(b, d, 0), 0.0)
        b = a * b_sh + b
        a = a * a_sh
        d *= 2
    h = a * hc_ref[...] + b
    hc_ref[...] = h[t - 1:t]
    oa_ref[0] = _gelu_tanh(ga_ref[0]) * h

    @pl.when(l == pl.num_programs(1) - 1)
    def _():
        hl_ref[0] = h[t - 1:t]
        nb_ref[0] = xa_ref[0, t - (CONV_W - 1):t, :]


def _lru(xa, ga, buf, h0, cw, cb, wr, br, wi, bi, lam):
    bsz, length, w = xa.shape
    t = _row_tile(length, 256)
    seq = pl.BlockSpec((1, t, w), lambda b, l: (b, l, 0))
    vec = _full((1, w))
    return pl.pallas_call(
        functools.partial(_lru_kernel, t=t),
        grid=(bsz, length // t),
        in_specs=[
            seq, seq,
            pl.BlockSpec((1, CONV_W - 1, w), lambda b, l: (b, 0, 0)),
            pl.BlockSpec((1, 1, w), lambda b, l: (b, 0, 0)),
            _full((CONV_W, w)), vec, _full((w, w)), vec, _full((w, w)), vec, vec,
        ],
        out_specs=[
            seq,
            pl.BlockSpec((1, 1, w), lambda b, l: (b, 0, 0)),
            pl.BlockSpec((1, CONV_W - 1, w), lambda b, l: (b, 0, 0)),
        ],
        out_shape=[
            jax.ShapeDtypeStruct((bsz, length, w), F32),
            jax.ShapeDtypeStruct((bsz, 1, w), F32),
            jax.ShapeDtypeStruct((bsz, CONV_W - 1, w), F32),
        ],
        scratch_shapes=[pltpu.VMEM((1, w), F32), pltpu.VMEM((SUBLANES, w), F32)],
        compiler_params=_cparams("parallel", "arbitrary"),
        name="lru",
    )(xa, ga, buf, h0, cw, cb, wr, br, wi, bi, lam)


def _sb_block(qh, kb, vb, tri, c, acc, mask):
    z = _dot_nt(qh, kb)
    l = -_softplus(z)
    if mask is not None:
        l = jnp.where(mask, l, 0.0)
    l_hi = l.astype(BF16)
    l_lo = (l - l_hi.astype(F32)).astype(BF16)
    incl = _dot(l_hi, tri) + _dot(l_lo, tri)
    e = jnp.exp(z + c + incl)
    if mask is not None:
        e = jnp.where(mask, e, 0.0)
    acc = acc + _dot(e.astype(BF16), vb)
    return c + incl[:, 0:1], acc


def _tri(n):
    r = lax.broadcasted_iota(jnp.int32, (n, n), 0)
    c = lax.broadcasted_iota(jnp.int32, (n, n), 1)
    return jnp.where(r >= c, 1.0, 0.0).astype(BF16)


def _sb_kernel(*refs, tq, tkp, n_past):
    if n_past:
        q_ref, kn_ref, vn_ref, kp_ref, vp_ref, o_ref = refs
    else:
        q_ref, kn_ref, vn_ref, o_ref = refs
    i = pl.program_id(2)
    q = q_ref[0] * (HEAD_B ** -0.5)
    lane = lax.broadcasted_iota(jnp.int32, (1, HEAD_PAIR), 1)
    r = lax.broadcasted_iota(jnp.int32, (tq, tq), 0)
    s = lax.broadcasted_iota(jnp.int32, (tq, tq), 1)
    causal = s < r
    tri_n = _tri(tq)
    tri_p = _tri(tkp) if n_past else None

    def new_block(j):
        rows = pl.ds(pl.multiple_of(j * tq, tq), tq)
        return kn_ref[0, rows, :].astype(BF16), vn_ref[0, rows, :].astype(BF16)

    outs = []
    for hd in range(2):
        qh = jnp.where((lane >= HEAD_B) == bool(hd), q, 0.0).astype(BF16)
        kb, vb = new_block(i)
        c, acc = _sb_block(qh, kb, vb, tri_n, jnp.zeros((tq, 1), F32), jnp.zeros((tq, HEAD_PAIR), F32), causal)

        def new_body(n, carry):
            kb, vb = new_block(i - 1 - n)
            return _sb_block(qh, kb, vb, tri_n, *carry, None)

        c, acc = lax.fori_loop(0, i, new_body, (c, acc))
        if n_past:
            def past_body(n, carry):
                rows = pl.ds(pl.multiple_of((n_past - 1 - n) * tkp, tkp), tkp)
                kb = kp_ref[0, rows, :].astype(BF16)
                vb = vp_ref[0, rows, :].astype(BF16)
                return _sb_block(qh, kb, vb, tri_p, *carry, None)

            c, acc = lax.fori_loop(0, n_past, past_body, (c, acc))
        outs.append(acc)
    o_ref[0] = jnp.where(lane < HEAD_B, outs[0], outs[1])


def _stick_breaking(q, k_new, v_new, k_past=None, v_past=None):
    bsz, length, w = q.shape
    tq = _row_tile(length, 128)
    tkp = 128
    n_past = 0 if k_past is None else k_past.shape[1] // tkp
    qspec = pl.BlockSpec((1, tq, HEAD_PAIR), lambda b, h, i: (b, i, h))
    new_spec = pl.BlockSpec((1, length, HEAD_PAIR), lambda b, h, i: (b, 0, h))
    in_specs = [qspec, new_spec, new_spec]
    args = [q, k_new, v_new]
    if n_past:
        past_spec = pl.BlockSpec((1, k_past.shape[1], HEAD_PAIR), lambda b, h, i: (b, 0, h))
        in_specs += [past_spec, past_spec]
        args += [k_past, v_past]
    return pl.pallas_call(
        functools.partial(_sb_kernel, tq=tq, tkp=tkp, n_past=n_past),
        grid=(bsz, w // HEAD_PAIR, length // tq),
        in_specs=in_specs,
        out_specs=qspec,
        out_shape=jax.ShapeDtypeStruct((bsz, length, w), F32),
        compiler_params=_cparams("parallel", "parallel", "arbitrary"),
        name="stick_breaking",
    )(*args)


def _ab_out_kernel(x_ref, oa_ref, ob_ref, g_ref, w_ref, o_ref, *, w):
    y = _dot(oa_ref[...].astype(BF16), w_ref[0:w, :]) + _dot(ob_ref[...].astype(BF16), w_ref[w:2 * w, :])
    o_ref[...] = x_ref[...] + _rms(y, g_ref[...])


def _ab_out(x, oa, ob, g, w_out):
    n, d = x.shape
    w = oa.shape[1]
    tm = _row_tile(n, 1024)
    return pl.pallas_call(
        functools.partial(_ab_out_kernel, w=w),
        grid=(n // tm,),
        in_specs=[
            pl.BlockSpec((tm, d), lambda i: (i, 0)),
            pl.BlockSpec((tm, w), lambda i: (i, 0)),
            pl.BlockSpec((tm, w), lambda i: (i, 0)),
            _full((1, d)),
            _full((2 * w, d)),
        ],
        out_specs=pl.BlockSpec((tm, d), lambda i: (i, 0)),
        out_shape=jax.ShapeDtypeStruct((n, d), F32),
        compiler_params=_cparams("parallel"),
        name="ab_out",
    )(x, oa, ob, g, w_out)


def _ret_in_kernel(x_ref, g_ref, w_ref, cos_ref, sin_ref, o_ref, hn_ref, *, dk):
    j = pl.program_id(1)

    @pl.when(j == 0)
    def _():
        hn_ref[...] = _rms(x_ref[...], g_ref[...]).astype(BF16)

    y = _dot(hn_ref[...], w_ref[...])

    @pl.when(j >= 2)
    def _():
        o_ref[...] = y

    @pl.when(j < 2)
    def _():
        scale = jnp.where(j == 0, 1.0, dk ** -0.5)
        cos = cos_ref[...] * scale
        sin = sin_ref[...] * scale
        half = dk // 2
        for h in range(y.shape[1] // dk):
            x1 = y[:, h * dk:h * dk + half]
            x2 = y[:, h * dk + half:(h + 1) * dk]
            o_ref[:, h * dk:h * dk + half] = x1 * cos - x2 * sin
            o_ref[:, h * dk + half:(h + 1) * dk] = x2 * cos + x1 * sin


def _ret_in(x, g, w_in, cos, sin, dk):
    n, d = x.shape
    cols = w_in.shape[1]
    tm = _row_tile(min(n, cos.shape[0]), 1024)
    nmod = cos.shape[0] // tm
    tab = pl.BlockSpec((tm, dk // 2), lambda i, j: (i % nmod, 0))
    return pl.pallas_call(
        functools.partial(_ret_in_kernel, dk=dk),
        grid=(n // tm, cols // d),
        in_specs=[
            pl.BlockSpec((tm, d), lambda i, j: (i, 0)),
            _full((1, d)),
            pl.BlockSpec((d, d), lambda i, j: (0, j)),
            tab, tab,
        ],
        out_specs=pl.BlockSpec((tm, d), lambda i, j: (i, j)),
        out_shape=jax.ShapeDtypeStruct((n, cols), F32),
        scratch_shapes=[pltpu.VMEM((tm, d), BF16)],
        compiler_params=_cparams("parallel", "arbitrary"),
        name="ret_in",
    )(x, g, w_in, cos, sin)


def _ret_kernel(*refs, t, has_state):
    if has_state:
        lg_ref, q_ref, k_ref, v_ref, r0_ref, o_ref, rl_ref, r_ref = refs
    else:
        lg_ref, q_ref, k_ref, v_ref, o_ref, rl_ref, r_ref = refs
    l = pl.program_id(2)

    @pl.when(l == 0)
    def _():
        r_ref[...] = r0_ref[0, 0] if has_state else jnp.zeros_like(r_ref)

    lg = lg_ref[pl.program_id(1)]
    q = q_ref[0].astype(BF16)
    k = k_ref[0]
    v = v_ref[0].astype(BF16)
    ri = lax.broadcasted_iota(jnp.int32, (t, t), 0)
    ci = lax.broadcasted_iota(jnp.int32, (t, t), 1)
    decay = jnp.exp(lg * jnp.abs(ri - ci).astype(F32))
    visible = (ci >> CHUNK_BITS) <= (ri >> CHUNK_BITS)
    s = jnp.where(visible, _dot_nt(q, k.astype(BF16)) * decay, 0.0)
    pos = lax.broadcasted_iota(jnp.int32, (t, 1), 0).astype(F32)
    r = r_ref[...]
    o_ref[0] = _dot(s.astype(BF16), v) + _dot(q, r.astype(BF16)) * jnp.exp(lg * (pos + 1.0))
    kz = (k * jnp.exp(lg * (t - 1.0 - pos))).astype(BF16)
    r_new = jnp.exp(jnp.full((1, 1), t, F32) * lg) * r + _dot_tn(kz, v)
    r_ref[...] = r_new

    @pl.when(l == pl.num_programs(2) - 1)
    def _():
        rl_ref[0, 0] = r_new


def _retention(proj, r0, log_g, bsz, length, hc, dk, dv):
    t = _row_tile(length, 256)
    kq = hc
    vq = 2 * hc * dk // dv
    in_specs = [
        pl.BlockSpec(memory_space=pltpu.SMEM),
        pl.BlockSpec((1, t, dk), lambda b, h, l: (b, l, h)),
        pl.BlockSpec((1, t, dk), lambda b, h, l: (b, l, kq + h)),
        pl.BlockSpec((1, t, dv), lambda b, h, l: (b, l, vq + h)),
    ]
    args = [log_g, proj, proj, proj]
    state_spec = pl.BlockSpec((1, 1, dk, dv), lambda b, h, l: (b, h, 0, 0))
    if r0 is not None:
        in_specs.append(state_spec)
        args.append(r0)
    return pl.pallas_call(
        functools.partial(_ret_kernel, t=t, has_state=r0 is not None),
        grid=(bsz, hc, length // t),
        in_specs=in_specs,
        out_specs=[pl.BlockSpec((1, t, dv), lambda b, h, l: (b, l, h)), state_spec],
        out_shape=[
            jax.ShapeDtypeStruct((bsz, length, hc * dv), F32),
            jax.ShapeDtypeStruct((bsz, hc, dk, dv), F32),
        ],
        scratch_shapes=[pltpu.VMEM((dk, dv), F32)],
        compiler_params=_cparams("parallel", "parallel", "arbitrary"),
        name="retention",
    )(*args)


def _ret_out_kernel(x_ref, o_ref_in, gate_ref, g_ref, w_ref, out_ref, *, hc, dv):
    y = None
    for h in range(hc):
        sl = slice(h * dv, (h + 1) * dv)
        o = o_ref_in[:, sl]
        mu = jnp.mean(o, axis=-1, keepdims=True)
        oc = o - mu
        on = oc * lax.rsqrt(jnp.mean(oc * oc, axis=-1, keepdims=True) + EPS)
        gt = gate_ref[:, sl]
        d = _dot((gt * _sigmoid(gt) * on).astype(BF16), w_ref[sl, :])
        y = d if y is None else y + d
    out_ref[...] = x_ref[...] + _rms(y, g_ref[...])


def _ret_out(x, o, proj, g, w_out, hc, dv):
    n, d = x.shape
    wv = hc * dv
    tm = _row_tile(n, 512)
    gate_block = proj.shape[1] // wv - 1
    return pl.pallas_call(
        functools.partial(_ret_out_kernel, hc=hc, dv=dv),
        grid=(n // tm,),
        in_specs=[
            pl.BlockSpec((tm, d), lambda i: (i, 0)),
            pl.BlockSpec((tm, wv), lambda i: (i, 0)),
            pl.BlockSpec((tm, wv), lambda i: (i, gate_block)),
            _full((1, d)),
            _full((wv, d)),
        ],
        out_specs=pl.BlockSpec((tm, d), lambda i: (i, 0)),
        out_shape=jax.ShapeDtypeStruct((n, d), F32),
        compiler_params=_cparams("parallel"),
        name="ret_out",
    )(x, o, proj, g, w_out)


def _block_diag(w):
    h, bi, bj = w.shape
    eye = jnp.eye(h, dtype=w.dtype)
    return (eye[:, None, :, None] * w[:, :, None, :]).reshape(h * bi, h * bj)


def _run_group(x, p, conv_state, lru_state, past_k, past_v, ret_state, pos0, wts):
    bsz, length, d = x.shape
    n = bsz * length
    depth = p.shape[0]
    x = x.reshape(n, d)
    hc, dk, dv = wts["hc"], wts["dk"], wts["dv"]
    pos = (pos0 + jnp.arange(length)).astype(F32)
    half = dk // 2
    freq = ROPE_BASE ** (-jnp.arange(half, dtype=F32) / half)
    ang = pos[:, None] * freq[None, :]
    reps = max(1, min(n, 1024) // length)
    cos = jnp.tile(jnp.cos(ang), (reps, 1))
    sin = jnp.tile(jnp.sin(ang), (reps, 1))
    new_h, new_conv, new_k, new_v, new_ret = [], [], [], [], []
    for i in range(depth):
        g = wts["norm_g"][i]
        j = i // 2
        x = _ffn(x, g[0:1], g[1:2], wts["ffn_w_gate"][i][0], wts["ffn_w_up"][i][0], wts["ffn_w_down"][i][0])
        if i % 2 == 0:
            xa, ga, q, k, v = _ab_in(x, g[2:3], wts["ab_w_in"][j])
            w = xa.shape[1]
            seq = lambda a: a.reshape(bsz, length, w)
            buf = jnp.zeros((bsz, CONV_W - 1, w), F32) if conv_state is None else conv_state[j]
            h0 = jnp.zeros((bsz, w), F32) if lru_state is None else lru_state[j]
            oa, h_last, nbuf = _lru(
                seq(xa), seq(ga), buf, h0.reshape(bsz, 1, w),
                wts["lru_conv_w"][j], wts["lru_conv_b"][j][None], wts["lru_w_r"][j], wts["lru_b_r"][j][None],
                wts["lru_w_i"][j], wts["lru_b_i"][j][None], wts["lru_lambda"][j][None])
            if past_k is None:
                ob = _stick_breaking(seq(q), seq(k), seq(v))
            else:
                plen = past_k.shape[2]
                ob = _stick_breaking(seq(q), seq(k), seq(v),
                                     past_k[j].reshape(bsz, plen, w), past_v[j].reshape(bsz, plen, w))
            x = _ab_out(x, oa.reshape(n, w), ob.reshape(n, w), g[3:4], wts["ab_w_out"][j])
            new_h.append(h_last.reshape(bsz, w))
            new_conv.append(nbuf)
            new_k.append(k.reshape(bsz, length, w // HEAD_B, HEAD_B))
            new_v.append(v.reshape(bsz, length, w // HEAD_B, HEAD_B))
        else:
            proj = _ret_in(x, g[2:3], wts["ret_w_in"][j], cos, sin, dk)
            r0 = None if ret_state is None else ret_state[j]
            o, r_last = _retention(proj.reshape(bsz, length, -1), r0, wts["log_g"], bsz, length, hc, dk, dv)
            x = _ret_out(x, o.reshape(n, hc * dv), proj, g[3:4], wts["ret_w_out"][j], hc, dv)
            new_ret.append(r_last)
        x = _ffn(x, g[4:5], g[5:6], wts["ffn_w_gate"][i][1], wts["ffn_w_up"][i][1], wts["ffn_w_down"][i][1])
        x = _ple(x, p[i].reshape(n, -1), g[6:7], g[7:8], wts["ple_w_gate"][i], wts["ple_w_in"][i])
    return (x.reshape(bsz, length, d), jnp.stack(new_h), jnp.stack(new_conv), jnp.stack(new_k),
            jnp.stack(new_v), jnp.stack(new_ret))


def kernel(x_prompt, x_sample, p_prompt, p_sample, state_lru_h, state_conv, cache_sb_k, cache_sb_v, state_ret,
           norm_g, ffn_w_gate, ffn_w_up, ffn_w_down, ple_w_in, ple_w_gate,
           ab_w_in, ab_w_out, lru_conv_w, lru_conv_b, lru_w_r, lru_b_r, lru_w_i, lru_b_i, lru_lambda,
           ret_w_in, ret_w_out):
    hc, dk, dv = state_ret.shape[2:]
    wts = dict(
        hc=hc, dk=dk, dv=dv,
        norm_g=norm_g,
        ffn_w_gate=ffn_w_gate.astype(BF16), ffn_w_up=ffn_w_up.astype(BF16), ffn_w_down=ffn_w_down.astype(BF16),
        ple_w_in=ple_w_in.astype(BF16), ple_w_gate=ple_w_gate.astype(BF16),
        ab_w_in=ab_w_in.astype(BF16), ab_w_out=ab_w_out.astype(BF16),
        lru_conv_w=lru_conv_w, lru_conv_b=lru_conv_b,
        lru_w_r=jax.vmap(_block_diag)(lru_w_r).astype(BF16), lru_b_r=lru_b_r,
        lru_w_i=jax.vmap(_block_diag)(lru_w_i).astype(BF16), lru_b_i=lru_b_i,
        lru_lambda=lru_lambda,
        ret_w_in=ret_w_in.astype(BF16), ret_w_out=ret_w_out.astype(BF16),
        log_g=jnp.log(1.0 - 2.0 ** (-5.0 - jnp.arange(hc, dtype=F32))),
    )
    y_p, h_p, c_p, k_p, v_p, r_p = _run_group(x_prompt, p_prompt, None, None, None, None, None, 0, wts)
    y_s, h_s, c_s, k_s, v_s, r_s = _run_group(
        x_sample, p_sample, state_conv, state_lru_h, cache_sb_k, cache_sb_v, state_ret, cache_sb_k.shape[2], wts)
    return (y_p, y_s, h_p, c_p, k_p, v_p, r_p, h_s, c_s, k_s, v_s, r_s)
```

```python
import functools

import jax
import jax.numpy as jnp
from jax import lax
from jax.experimental import pallas as pl
from jax.experimental.pallas import tpu as pltpu

F32 = jnp.float32
BF16 = jnp.bfloat16

EPS = 1e-6
LRU_C = 8.0
CONV_W = 4
CHUNK_BITS = 6
ROPE_BASE = 10000.0
HEAD_B = 64
HEAD_PAIR = 2 * HEAD_B
SB_DEAD = -40.0
SUBLANES = 8
FF_SUB = 256
VMEM_LIMIT_BYTES = 56 * 1024 * 1024
N_NORMS = 8


def _cparams(*sem):
    return pltpu.CompilerParams(dimension_semantics=sem, vmem_limit_bytes=VMEM_LIMIT_BYTES)


def _dot(a, b):
    return jnp.dot(a, b, preferred_element_type=F32)


def _dot_nt(a, b):
    return lax.dot_general(a, b, (((1,), (1,)), ((), ())), preferred_element_type=F32)


def _dot_tn(a, b):
    return lax.dot_general(a, b, (((0,), (0,)), ((), ())), preferred_element_type=F32)


def _rms(x, g):
    return x * lax.rsqrt(jnp.mean(x * x, axis=-1, keepdims=True) + EPS) * g


def _sigmoid(x):
    return 1.0 / (1.0 + jnp.exp(-x))


def _softplus(x):
    return jnp.maximum(x, 0.0) + jnp.log(1.0 + jnp.exp(-jnp.abs(x)))


def _gelu_tanh(x):
    return 0.5 * x * (1.0 + jnp.tanh(0.7978845608028654 * (x + 0.044715 * x * x * x)))


def _row_tile(n, pref):
    t = min(n, pref)
    while n % t:
        t //= 2
    return t


def _layer_spec(arr, layer, pipeline_mode=None):
    layer = layer if isinstance(layer, tuple) else (layer,)
    rest = arr.shape[len(layer):]
    idx = layer + (0,) * len(rest)
    return pl.BlockSpec((None,) * len(layer) + rest, lambda *_: idx, pipeline_mode=pipeline_mode)


def _ffn_kernel(x_ref, g_ref, wg_ref, wu_ref, wd_ref, o_ref, acc_ref, *, n_sub, pre, post):
    xn = _rms(x_ref[...], g_ref[pre:pre + 1, :]).astype(BF16)
    for c in range(n_sub):
        sl = pl.ds(c * FF_SUB, FF_SUB)
        h = _dot(xn, wg_ref[:, sl])
        u = _dot(xn, wu_ref[:, sl])
        a = (h * _sigmoid(h) * u).astype(BF16)
        d = _dot(a, wd_ref[sl, :])
        if c == 0:
            acc_ref[...] = d
        else:
            acc_ref[...] += d
    o_ref[...] = x_ref[...] + 0.5 * _rms(acc_ref[...], g_ref[post:post + 1, :])


def _ffn(x, norm_g, wg, wu, wd, layer, which, pre, post):
    n, d = x.shape
    f = wg.shape[-1]
    tm = _row_tile(n, 1024)
    once = pl.Buffered(1)
    return pl.pallas_call(
        functools.partial(_ffn_kernel, n_sub=f // FF_SUB, pre=pre, post=post),
        grid=(n // tm,),
        in_specs=[
            pl.BlockSpec((tm, d), lambda i: (i, 0)),
            _layer_spec(norm_g, layer),
            _layer_spec(wg, (layer, which), once),
            _layer_spec(wu, (layer, which), once),
            _layer_spec(wd, (layer, which), once),
        ],
        out_specs=pl.BlockSpec((tm, d), lambda i: (i, 0)),
        out_shape=jax.ShapeDtypeStruct((n, d), F32),
        scratch_shapes=[pltpu.VMEM((tm, d), F32)],
        compiler_params=_cparams("parallel"),
        name="ffn",
    )(x, norm_g, wg, wu, wd)


def _ple_kernel(x_ref, p_ref, g_ref, wgate_ref, win_ref, o_ref, *, pre, post):
    x = x_ref[...]
    gate = _sigmoid(_dot(_rms(x, g_ref[pre:pre + 1, :]).astype(BF16), wgate_ref[...]))
    e = _dot(p_ref[...].astype(BF16), win_ref[...])
    o_ref[...] = x + _rms(gate * e, g_ref[post:post + 1, :])


def _ple(x, p, norm_g, wgate, win, layer, pre, post):
    n, d = x.shape
    pd = p.shape[-1]
    tm = _row_tile(n, 1024)
    return pl.pallas_call(
        functools.partial(_ple_kernel, pre=pre, post=post),
        grid=(n // tm,),
        in_specs=[
            pl.BlockSpec((tm, d), lambda i: (i, 0)),
            pl.BlockSpec((None, tm, pd), lambda i: (layer, i, 0)),
            _layer_spec(norm_g, layer),
            _layer_spec(wgate, layer),
            _layer_spec(win, layer),
        ],
        out_specs=pl.BlockSpec((tm, d), lambda i: (i, 0)),
        out_shape=jax.ShapeDtypeStruct((n, d), F32),
        compiler_params=_cparams("parallel"),
        name="ple",
    )(x, p, norm_g, wgate, win)


def _ab_in_kernel(x_ref, g_ref, w_ref, xa_ref, ga_ref, q_ref, k_ref, v_ref, *, w, pre):
    hn = _rms(x_ref[...], g_ref[pre:pre + 1, :]).astype(BF16)
    for idx, o_ref in enumerate((xa_ref, ga_ref, q_ref, k_ref, v_ref)):
        o_ref[...] = _dot(hn, w_ref[:, idx * w:(idx + 1) * w])


def _ab_in(x, norm_g, w_in, layer, j, pre):
    n, d = x.shape
    w = w_in.shape[-1] // 5
    tm = _row_tile(n, 1024)
    out = jax.ShapeDtypeStruct((n, w), F32)
    ospec = pl.BlockSpec((tm, w), lambda i: (i, 0))
    return pl.pallas_call(
        functools.partial(_ab_in_kernel, w=w, pre=pre),
        grid=(n // tm,),
        in_specs=[pl.BlockSpec((tm, d), lambda i: (i, 0)), _layer_spec(norm_g, layer), _layer_spec(w_in, j)],
        out_specs=[ospec] * 5,
        out_shape=[out] * 5,
        compiler_params=_cparams("parallel"),
        name="ab_in",
    )(x, norm_g, w_in)


def _lru_kernel(xa_ref, ga_ref, buf_ref, h0_ref, cw_ref, cb_ref, wr_ref, br_ref, wi_ref, bi_ref, lam_ref,
                oa_ref, hl_ref, nb_ref, hc_ref, xc_ref, *, t):
    l = pl.program_id(1)

    @pl.when(l == 0)
    def _():
        hc_ref[...] = h0_ref[...]
        xc_ref[...] = jnp.zeros_like(xc_ref)
        xc_ref[SUBLANES - (CONV_W - 1):, :] = buf_ref[...]

    x = xa_ref[0]
    prev = xc_ref[...]
    row8 = lax.broadcasted_iota(jnp.int32, (SUBLANES, 1), 0)
    cw = cw_ref[...]
    y = cb_ref[...] + x * cw[CONV_W - 1:CONV_W]
    for k in range(1, CONV_W):
        xs = pltpu.roll(x, k, 0)
        head = jnp.where(row8 < k, pltpu.roll(prev, k, 0), xs[:SUBLANES])
        xs = jnp.concatenate([head, xs[SUBLANES:]], axis=0)
        y = y + xs * cw[CONV_W - 1 - k:CONV_W - k]
    xc_ref[...] = x[t - SUBLANES:]

    yb = y.astype(BF16)
    rg = _sigmoid(_dot(yb, wr_ref[...]) + br_ref[...])
    ig = _sigmoid(_dot(yb, wi_ref[...]) + bi_ref[...])
    log_a = (-LRU_C) * rg * _softplus(-lam_ref[...])
    a = jnp.exp(log_a)
    b = jnp.sqrt(1.0 - jnp.exp(2.0 * log_a)) * ig * y

    row = lax.broadcasted_iota(jnp.int32, (t, 1), 0)
    d = 1
    while d < t:
        keep = row >= d
        a_sh = jnp.where(keep, pltpu.roll(a, d, 0), 1.0)
        b_sh = jnp.where(keep, pltpu.roll(b, d, 0), 0.0)
        b = a * b_sh + b
        a = a * a_sh
        d *= 2
    h = a * hc_ref[...] + b
    hc_ref[...] = h[t - 1:t]
    oa_ref[0] = _gelu_tanh(ga_ref[0]) * h

    @pl.when(l == pl.num_programs(1) - 1)
    def _():
        hl_ref[0] = h[t - 1:t]
        nb_ref[0] = xa_ref[0, t - (CONV_W - 1):t, :]


def _lru(xa, ga, buf, h0, lp, j, js):
    bsz, length, w = xa.shape
    t = _row_tile(length, 256)
    seq = pl.BlockSpec((1, t, w), lambda b, l: (b, l, 0))
    par = lambda a: _layer_spec(a, j)
    return pl.pallas_call(
        functools.partial(_lru_kernel, t=t),
        grid=(bsz, length // t),
        in_specs=[
            seq, seq,
            pl.BlockSpec((None, None, CONV_W - 1, w), lambda b, l: (js, b, 0, 0)),
            pl.BlockSpec((None, None, 1, w), lambda b, l: (js, b, 0, 0)),
            par(lp["conv_w"]), par(lp["conv_b"]), par(lp["w_r"]), par(lp["b_r"]),
            par(lp["w_i"]), par(lp["b_i"]), par(lp["lam"]),
        ],
        out_specs=[
            seq,
            pl.BlockSpec((1, 1, w), lambda b, l: (b, 0, 0)),
            pl.BlockSpec((1, CONV_W - 1, w), lambda b, l: (b, 0, 0)),
        ],
        out_shape=[
            jax.ShapeDtypeStruct((bsz, length, w), F32),
            jax.ShapeDtypeStruct((bsz, 1, w), F32),
            jax.ShapeDtypeStruct((bsz, CONV_W - 1, w), F32),
        ],
        scratch_shapes=[pltpu.VMEM((1, w), F32), pltpu.VMEM((SUBLANES, w), F32)],
        compiler_params=_cparams("parallel", "arbitrary"),
        name="lru",
    )(xa, ga, buf, h0, lp["conv_w"], lp["conv_b"], lp["w_r"], lp["b_r"], lp["w_i"], lp["b_i"], lp["lam"])


def _sb_step(qh, k_rows, v_rows, tri, cs, accs, mask, low):
    n_pairs = len(qh)
    tq = accs[0].shape[0]
    lanes = lambda p: slice(p * HEAD_PAIR, (p + 1) * HEAD_PAIR)
    zs = [_dot_nt(qh[p], k_rows[:, lanes(p)].astype(BF16)) for p in range(n_pairs)]
    ls = [-_softplus(z) for z in zs]
    if mask is not None:
        ls = [jnp.where(mask, l, 0.0) for l in ls]
    his = [l.astype(BF16) for l in ls]
    los = [(l - hi.astype(F32)).astype(BF16) for l, hi in zip(ls, his)]
    sums = _dot(jnp.concatenate(his + los, axis=0), tri)
    m = n_pairs * 2 * tq
    cs_out, accs_out = [], []
    for p in range(n_pairs):
        incl = sums[p * 2 * tq:(p + 1) * 2 * tq] + sums[m + p * 2 * tq:m + (p + 1) * 2 * tq]
        e = jnp.exp(zs[p] + cs[p] + incl)
        if mask is not None:
            e = jnp.where(mask, e, 0.0)
        e = e.astype(BF16)
        e2 = jnp.concatenate([e[:tq], e[tq:]], axis=1)
        vp = v_rows[:, lanes(p)]
        v2 = jnp.concatenate([jnp.where(low, vp, 0.0), jnp.where(low, 0.0, vp)], axis=0).astype(BF16)
        accs_out.append(accs[p] + _dot(e2, v2))
        cs_out.append(cs[p] + incl[:, 0:1])
    return tuple(cs_out), tuple(accs_out)


def _tri(n):
    r = lax.broadcasted_iota(jnp.int32, (n, n), 0)
    c = lax.broadcasted_iota(jnp.int32, (n, n), 1)
    return jnp.where(r >= c, 1.0, 0.0).astype(BF16)


def _sb_kernel(*refs, tq, tkp, n_past, n_pairs):
    if n_past:
        q_ref, kn_ref, vn_ref, kp_ref, vp_ref, o_ref = refs
    else:
        q_ref, kn_ref, vn_ref, o_ref = refs
    i = pl.program_id(1)
    low = lax.broadcasted_iota(jnp.int32, (1, HEAD_PAIR), 1) < HEAD_B
    r2 = lax.broadcasted_iota(jnp.int32, (2 * tq, tq), 0)
    causal = lax.broadcasted_iota(jnp.int32, (2 * tq, tq), 1) < jnp.where(r2 >= tq, r2 - tq, r2)
    tri_n = _tri(tq)

    q = q_ref[0] * (HEAD_B ** -0.5)
    qh = []
    for p in range(n_pairs):
        qp = q[:, p * HEAD_PAIR:(p + 1) * HEAD_PAIR]
        qh.append(jnp.concatenate([jnp.where(low, qp, 0.0), jnp.where(low, 0.0, qp)], axis=0).astype(BF16))

    def step(k_rows, v_rows, tri, cs, accs, mask):
        return _sb_step(qh, k_rows, v_rows, tri, cs, accs, mask, low)

    def alive(cs):
        m = functools.reduce(jnp.maximum, cs)
        return (jnp.max(m) > SB_DEAD).astype(jnp.int32)

    def sweep(limit, fetch, tri, flag, cs, accs):
        def cond(st):
            return jnp.logical_and(st[0] < limit, st[1] > 0)

        def body(st):
            n, _, cs, accs = st
            k_rows, v_rows = fetch(n)
            cs, accs = step(k_rows, v_rows, tri, cs, accs, None)
            return n + 1, alive(cs), cs, accs

        _, flag, cs, accs = lax.while_loop(cond, body, (jnp.int32(0), flag, cs, accs))
        return flag, cs, accs

    def fetch_new(n):
        rows = pl.ds(pl.multiple_of((i - 1 - n) * tq, tq), tq)
        return kn_ref[0, rows, :], vn_ref[0, rows, :]

    def fetch_past(n):
        rows = pl.ds(pl.multiple_of((n_past - 1 - n) * tkp, tkp), tkp)
        return kp_ref[rows, :], vp_ref[rows, :]

    own = pl.ds(pl.multiple_of(i * tq, tq), tq)
    cs = tuple(jnp.zeros((2 * tq, 1), F32) for _ in range(n_pairs))
    accs = tuple(jnp.zeros((tq, HEAD_PAIR), F32) for _ in range(n_pairs))
    cs, accs = step(kn_ref[0, own, :], vn_ref[0, own, :], tri_n, cs, accs, causal)
    flag, cs, accs = sweep(i, fetch_new, tri_n, alive(cs), cs, accs)
    if n_past:
        flag, cs, accs = sweep(n_past, fetch_past, _tri(tkp), flag, cs, accs)
    for p in range(n_pairs):
        o_ref[0, :, p * HEAD_PAIR:(p + 1) * HEAD_PAIR] = accs[p]


def _stick_breaking(q, k_new, v_new, past=None):
    bsz, length, w = q.shape
    tq = _row_tile(length, 128)
    tkp = 128
    qspec = pl.BlockSpec((1, tq, w), lambda b, i: (b, i, 0))
    new_spec = pl.BlockSpec((1, length, w), lambda b, i: (b, 0, 0))
    in_specs = [qspec, new_spec, new_spec]
    args = [q, k_new, v_new]
    n_past = 0
    if past is not None:
        k_past, v_past, j = past
        plen = k_past.shape[2]
        n_past = plen // tkp
        past_spec = pl.BlockSpec((None, None, plen, w), lambda b, i: (j, b, 0, 0))
        in_specs += [past_spec, past_spec]
        args += [k_past, v_past]
    return pl.pallas_call(
        functools.partial(_sb_kernel, tq=tq, tkp=tkp, n_past=n_past, n_pairs=w // HEAD_PAIR),
        grid=(bsz, length // tq),
        in_specs=in_specs,
        out_specs=qspec,
        out_shape=jax.ShapeDtypeStruct((bsz, length, w), F32),
        compiler_params=_cparams("parallel", "arbitrary"),
        name="stick_breaking",
    )(*args)


def _ab_out_kernel(x_ref, oa_ref, ob_ref, g_ref, w_ref, o_ref, *, w, post):
    y = _dot(oa_ref[...].astype(BF16), w_ref[0:w, :]) + _dot(ob_ref[...].astype(BF16), w_ref[w:2 * w, :])
    o_ref[...] = x_ref[...] + _rms(y, g_ref[post:post + 1, :])


def _ab_out(x, oa, ob, norm_g, w_out, layer, j, post):
    n, d = x.shape
    w = oa.shape[1]
    tm = _row_tile(n, 1024)
    return pl.pallas_call(
        functools.partial(_ab_out_kernel, w=w, post=post),
        grid=(n // tm,),
        in_specs=[
            pl.BlockSpec((tm, d), lambda i: (i, 0)),
            pl.BlockSpec((tm, w), lambda i: (i, 0)),
            pl.BlockSpec((tm, w), lambda i: (i, 0)),
            _layer_spec(norm_g, layer),
            _layer_spec(w_out, j),
        ],
        out_specs=pl.BlockSpec((tm, d), lambda i: (i, 0)),
        out_shape=jax.ShapeDtypeStruct((n, d), F32),
        compiler_params=_cparams("parallel"),
        name="ab_out",
    )(x, oa, ob, norm_g, w_out)


def _ret_in_kernel(x_ref, g_ref, w_ref, cos_ref, sin_ref, o_ref, hn_ref, *, dk, pre):
    j = pl.program_id(1)

    @pl.when(j == 0)
    def _():
        hn_ref[...] = _rms(x_ref[...], g_ref[pre:pre + 1, :]).astype(BF16)

    y = _dot(hn_ref[...], w_ref[...])

    @pl.when(j >= 2)
    def _():
        o_ref[...] = y

    @pl.when(j < 2)
    def _():
        scale = jnp.where(j == 0, 1.0, dk ** -0.5)
        cos = cos_ref[...] * scale
        sin = sin_ref[...] * scale
        half = dk // 2
        for h in range(y.shape[1] // dk):
            x1 = y[:, h * dk:h * dk + half]
            x2 = y[:, h * dk + half:(h + 1) * dk]
            o_ref[:, h * dk:h * dk + half] = x1 * cos - x2 * sin
            o_ref[:, h * dk + half:(h + 1) * dk] = x2 * cos + x1 * sin


def _ret_in(x, norm_g, w_in, cos, sin, dk, layer, j, pre):
    n, d = x.shape
    cols = w_in.shape[-1]
    tm = _row_tile(min(n, cos.shape[0]), 1024)
    nmod = cos.shape[0] // tm
    tab = pl.BlockSpec((tm, dk // 2), lambda i, c: (i % nmod, 0))
    return pl.pallas_call(
        functools.partial(_ret_in_kernel, dk=dk, pre=pre),
        grid=(n // tm, cols // d),
        in_specs=[
            pl.BlockSpec((tm, d), lambda i, c: (i, 0)),
            _layer_spec(norm_g, layer),
            pl.BlockSpec((None, d, d), lambda i, c: (j, 0, c)),
            tab, tab,
        ],
        out_specs=pl.BlockSpec((tm, d), lambda i, c: (i, c)),
        out_shape=jax.ShapeDtypeStruct((n, cols), F32),
        scratch_shapes=[pltpu.VMEM((tm, d), BF16)],
        compiler_params=_cparams("parallel", "arbitrary"),
        name="ret_in",
    )(x, norm_g, w_in, cos, sin)


def _ret_kernel(*refs, t, has_state):
    if has_state:
        lg_ref, q_ref, k_ref, v_ref, r0_ref, o_ref, rl_ref, r_ref = refs
    else:
        lg_ref, q_ref, k_ref, v_ref, o_ref, rl_ref, r_ref = refs
    l = pl.program_id(2)

    @pl.when(l == 0)
    def _():
        r_ref[...] = r0_ref[...] if has_state else jnp.zeros_like(r_ref)

    lg = lg_ref[pl.program_id(1)]
    q = q_ref[0].astype(BF16)
    k = k_ref[0]
    v = v_ref[0].astype(BF16)
    ri = lax.broadcasted_iota(jnp.int32, (t, t), 0)
    ci = lax.broadcasted_iota(jnp.int32, (t, t), 1)
    decay = jnp.exp(lg * jnp.abs(ri - ci).astype(F32))
    visible = (ci >> CHUNK_BITS) <= (ri >> CHUNK_BITS)
    s = jnp.where(visible, _dot_nt(q, k.astype(BF16)) * decay, 0.0)
    pos = lax.broadcasted_iota(jnp.int32, (t, 1), 0).astype(F32)
    r = r_ref[...]
    o_ref[0] = _dot(s.astype(BF16), v) + _dot(q, r.astype(BF16)) * jnp.exp(lg * (pos + 1.0))
    kz = (k * jnp.exp(lg * (t - 1.0 - pos))).astype(BF16)
    r_new = jnp.exp(jnp.full((1, 1), t, F32) * lg) * r + _dot_tn(kz, v)
    r_ref[...] = r_new

    @pl.when(l == pl.num_programs(2) - 1)
    def _():
        rl_ref[0, 0] = r_new


def _retention(proj, state, log_g, bsz, length, hc, dk, dv):
    t = _row_tile(length, 256)
    kq = hc
    vq = 2 * hc * dk // dv
    in_specs = [
        pl.BlockSpec(memory_space=pltpu.SMEM),
        pl.BlockSpec((1, t, dk), lambda b, h, l: (b, l, h)),
        pl.BlockSpec((1, t, dk), lambda b, h, l: (b, l, kq + h)),
        pl.BlockSpec((1, t, dv), lambda b, h, l: (b, l, vq + h)),
    ]
    args = [log_g, proj, proj, proj]
    if state is not None:
        states, j = state
        in_specs.append(pl.BlockSpec((None, None, None, dk, dv), lambda b, h, l: (j, b, h, 0, 0)))
        args.append(states)
    return pl.pallas_call(
        functools.partial(_ret_kernel, t=t, has_state=state is not None),
        grid=(bsz, hc, length // t),
        in_specs=in_specs,
        out_specs=[
            pl.BlockSpec((1, t, dv), lambda b, h, l: (b, l, h)),
            pl.BlockSpec((1, 1, dk, dv), lambda b, h, l: (b, h, 0, 0)),
        ],
        out_shape=[
            jax.ShapeDtypeStruct((bsz, length, hc * dv), F32),
            jax.ShapeDtypeStruct((bsz, hc, dk, dv), F32),
        ],
        scratch_shapes=[pltpu.VMEM((dk, dv), F32)],
        compiler_params=_cparams("parallel", "parallel", "arbitrary"),
        name="retention",
    )(*args)


def _ret_out_kernel(x_ref, o_ref_in, gate_ref, g_ref, w_ref, out_ref, *, hc, dv, post):
    y = None
    for h in range(hc):
        sl = slice(h * dv, (h + 1) * dv)
        o = o_ref_in[:, sl]
        mu = jnp.mean(o, axis=-1, keepdims=True)
        oc = o - mu
        on = oc * lax.rsqrt(jnp.mean(oc * oc, axis=-1, keepdims=True) + EPS)
        gt = gate_ref[:, sl]
        d = _dot((gt * _sigmoid(gt) * on).astype(BF16), w_ref[sl, :])
        y = d if y is None else y + d
    out_ref[...] = x_ref[...] + _rms(y, g_ref[post:post + 1, :])


def _ret_out(x, o, proj, norm_g, w_out, hc, dv, layer, j, post):
    n, d = x.shape
    wv = hc * dv
    tm = _row_tile(n, 512)
    gate_block = proj.shape[1] // wv - 1
    return pl.pallas_call(
        functools.partial(_ret_out_kernel, hc=hc, dv=dv, post=post),
        grid=(n // tm,),
        in_specs=[
            pl.BlockSpec((tm, d), lambda i: (i, 0)),
            pl.BlockSpec((tm, wv), lambda i: (i, 0)),
            pl.BlockSpec((tm, wv), lambda i: (i, gate_block)),
            _layer_spec(norm_g, layer),
            _layer_spec(w_out, j),
        ],
        out_specs=pl.BlockSpec((tm, d), lambda i: (i, 0)),
        out_shape=jax.ShapeDtypeStruct((n, d), F32),
        compiler_params=_cparams("parallel"),
        name="ret_out",
    )(x, o, proj, norm_g, w_out)


def _block_diag(w):
    n, h, bi, bj = w.shape
    eye = jnp.eye(h, dtype=w.dtype)
    return (eye[None, :, None, :, None] * w[:, :, :, None, :]).reshape(n, h * bi, h * bj)


def _run_group(x, p, conv_state, lru_state, past_k, past_v, ret_state, pos0, wts):
    bsz, length, d = x.shape
    n = bsz * length
    depth = p.shape[0]
    x = x.reshape(n, d)
    p = p.reshape(depth, n, -1)
    norm_g = wts["norm_g"]
    hc, dk, dv = wts["hc"], wts["dk"], wts["dv"]
    w = wts["lru"]["lam"].shape[-1]
    pos = (pos0 + jnp.arange(length)).astype(F32)
    half = dk // 2
    freq = ROPE_BASE ** (-jnp.arange(half, dtype=F32) / half)
    ang = pos[:, None] * freq[None, :]
    reps = max(1, min(n, 1024) // length)
    cos = jnp.tile(jnp.cos(ang), (reps, 1))
    sin = jnp.tile(jnp.sin(ang), (reps, 1))
    fresh = conv_state is None
    if fresh:
        conv_state = jnp.zeros((1, bsz, CONV_W - 1, w), F32)
        lru_state = jnp.zeros((1, bsz, w), F32)
    lru_state = lru_state.reshape(-1, bsz, 1, w)
    if not fresh:
        plen = past_k.shape[2]
        past_k = past_k.reshape(-1, bsz, plen, w)
        past_v = past_v.reshape(-1, bsz, plen, w)
    seq = lambda a: a.reshape(bsz, length, -1)
    new_h, new_conv, new_k, new_v, new_ret = [], [], [], [], []
    for i in range(depth):
        j = i // 2
        js = 0 if fresh else j
        x = _ffn(x, norm_g, wts["ffn_w_gate"], wts["ffn_w_up"], wts["ffn_w_down"], i, 0, 0, 1)
        if i % 2 == 0:
            xa, ga, q, k, v = _ab_in(x, norm_g, wts["ab_w_in"], i, j, 2)
            oa, h_last, nbuf = _lru(seq(xa), seq(ga), conv_state, lru_state, wts["lru"], j, js)
            ob = _stick_breaking(seq(q), seq(k), seq(v), None if fresh else (past_k, past_v, j))
            x = _ab_out(x, oa.reshape(n, w), ob.reshape(n, w), norm_g, wts["ab_w_out"], i, j, 3)
            new_h.append(h_last.reshape(bsz, w))
            new_conv.append(nbuf)
            new_k.append(k.reshape(bsz, length, w // HEAD_B, HEAD_B))
            new_v.append(v.reshape(bsz, length, w // HEAD_B, HEAD_B))
        else:
            proj = _ret_in(x, norm_g, wts["ret_w_in"], cos, sin, dk, i, j, 2)
            o, r_last = _retention(seq(proj), None if fresh else (ret_state, j), wts["log_g"],
                                   bsz, length, hc, dk, dv)
            x = _ret_out(x, o.reshape(n, hc * dv), proj, norm_g, wts["ret_w_out"], hc, dv, i, j, 3)
            new_ret.append(r_last)
        x = _ffn(x, norm_g, wts["ffn_w_gate"], wts["ffn_w_up"], wts["ffn_w_down"], i, 1, 4, 5)
        x = _ple(x, p, norm_g, wts["ple_w_gate"], wts["ple_w_in"], i, 6, 7)
    return (x.reshape(bsz, length, d), jnp.stack(new_h), jnp.stack(new_conv), jnp.stack(new_k),
            jnp.stack(new_v), jnp.stack(new_ret))


def kernel(x_prompt, x_sample, p_prompt, p_sample, state_lru_h, state_conv, cache_sb_k, cache_sb_v, state_ret,
           norm_g, ffn_w_gate, ffn_w_up, ffn_w_down, ple_w_in, ple_w_gate,
           ab_w_in, ab_w_out, lru_conv_w, lru_conv_b, lru_w_r, lru_b_r, lru_w_i, lru_b_i, lru_lambda,
           ret_w_in, ret_w_out):
    assert norm_g.shape[1] == N_NORMS
    hc, dk, dv = state_ret.shape[2:]
    row = lambda a: a[:, None, :]
    wts = dict(
        hc=hc, dk=dk, dv=dv,
        norm_g=norm_g,
        ffn_w_gate=ffn_w_gate.astype(BF16), ffn_w_up=ffn_w_up.astype(BF16), ffn_w_down=ffn_w_down.astype(BF16),
        ple_w_in=ple_w_in.astype(BF16), ple_w_gate=ple_w_gate.astype(BF16),
        ab_w_in=ab_w_in.astype(BF16), ab_w_out=ab_w_out.astype(BF16),
        lru=dict(
            conv_w=lru_conv_w, conv_b=row(lru_conv_b),
            w_r=_block_diag(lru_w_r).astype(BF16), b_r=row(lru_b_r),
            w_i=_block_diag(lru_w_i).astype(BF16), b_i=row(lru_b_i),
            lam=row(lru_lambda),
        ),
        ret_w_in=ret_w_in.astype(BF16), ret_w_out=ret_w_out.astype(BF16),
        log_g=jnp.log(1.0 - 2.0 ** (-5.0 - jnp.arange(hc, dtype=F32))),
    )
    y_p, h_p, c_p, k_p, v_p, r_p = _run_group(x_prompt, p_prompt, None, None, None, None, None, 0, wts)
    y_s, h_s, c_s, k_s, v_s, r_s = _run_group(
        x_sample, p_sample, state_conv, state_lru_h, cache_sb_k, cache_sb_v, state_ret, cache_sb_k.shape[2], wts)
    return (y_p, y_s, h_p, c_p, k_p, v_p, r_p, h_s, c_s, k_s, v_s, r_s)
```

```python
import functools

import jax
import jax.numpy as jnp
from jax import lax
from jax.experimental import pallas as pl
from jax.experimental.pallas import tpu as pltpu

F32 = jnp.float32
BF16 = jnp.bfloat16

EPS = 1e-6
LRU_C = 8.0
CONV_W = 4
CHUNK_BITS = 6
ROPE_BASE = 10000.0
HEAD_B = 64
HEAD_PAIR = 2 * HEAD_B
SB_DEAD = -40.0
SUBLANES = 8
FF_SUB = 256
VMEM_LIMIT_BYTES = 56 * 1024 * 1024
N_NORMS = 8


def _cparams(*sem):
    return pltpu.CompilerParams(dimension_semantics=sem, vmem_limit_bytes=VMEM_LIMIT_BYTES)


def _dot(a, b):
    return jnp.dot(a, b, preferred_element_type=F32)


def _dot_nt(a, b):
    return lax.dot_general(a, b, (((1,), (1,)), ((), ())), preferred_element_type=F32)


def _dot_tn(a, b):
    return lax.dot_general(a, b, (((0,), (0,)), ((), ())), preferred_element_type=F32)


def _rms(x, g):
    return x * lax.rsqrt(jnp.mean(x * x, axis=-1, keepdims=True) + EPS) * g


def _sigmoid(x):
    return 1.0 / (1.0 + jnp.exp(-x))


def _softplus(x):
    return jnp.maximum(x, 0.0) + jnp.log(1.0 + jnp.exp(-jnp.abs(x)))


def _gelu_tanh(x):
    return 0.5 * x * (1.0 + jnp.tanh(0.7978845608028654 * (x + 0.044715 * x * x * x)))


def _row_tile(n, pref):
    t = min(n, pref)
    while n % t:
        t //= 2
    return t


def _layer_spec(arr, layer, pipeline_mode=None):
    layer = layer if isinstance(layer, tuple) else (layer,)
    rest = arr.shape[len(layer):]
    idx = layer + (0,) * len(rest)
    return pl.BlockSpec((None,) * len(layer) + rest, lambda *_: idx, pipeline_mode=pipeline_mode)


def _ffn_kernel(*refs, n_sub, pre, post, ple_norms):
    if ple_norms is None:
        x_ref, g_ref, wg_ref, wu_ref, wd_ref, o_ref, acc_ref = refs
    else:
        x_ref, g_ref, wg_ref, wu_ref, wd_ref, p_ref, wgate_ref, win_ref, o_ref, acc_ref = refs
    xn = _rms(x_ref[...], g_ref[pre:pre + 1, :]).astype(BF16)
    for c in range(n_sub):
        sl = pl.ds(c * FF_SUB, FF_SUB)
        h = _dot(xn, wg_ref[:, sl])
        u = _dot(xn, wu_ref[:, sl])
        a = (h * _sigmoid(h) * u).astype(BF16)
        d = _dot(a, wd_ref[sl, :])
        if c == 0:
            acc_ref[...] = d
        else:
            acc_ref[...] += d
    x = x_ref[...] + 0.5 * _rms(acc_ref[...], g_ref[post:post + 1, :])
    if ple_norms is not None:
        pre2, post2 = ple_norms
        gate = _sigmoid(_dot(_rms(x, g_ref[pre2:pre2 + 1, :]).astype(BF16), wgate_ref[...]))
        e = _dot(p_ref[...].astype(BF16), win_ref[...])
        x = x + _rms(gate * e, g_ref[post2:post2 + 1, :])
    o_ref[...] = x


def _ffn(x, norm_g, wg, wu, wd, layer, which, pre, post, ple=None):
    n, d = x.shape
    f = wg.shape[-1]
    tm = _row_tile(n, 1024)
    once = pl.Buffered(1)
    in_specs = [
        pl.BlockSpec((tm, d), lambda i: (i, 0)),
        _layer_spec(norm_g, layer),
        _layer_spec(wg, (layer, which), once),
        _layer_spec(wu, (layer, which), once),
        _layer_spec(wd, (layer, which), once),
    ]
    args = [x, norm_g, wg, wu, wd]
    ple_norms = None
    if ple is not None:
        p, wgate, win, pre2, post2 = ple
        ple_norms = (pre2, post2)
        in_specs += [
            pl.BlockSpec((None, tm, p.shape[-1]), lambda i: (layer, i, 0)),
            _layer_spec(wgate, layer, once),
            _layer_spec(win, layer, once),
        ]
        args += [p, wgate, win]
    return pl.pallas_call(
        functools.partial(_ffn_kernel, n_sub=f // FF_SUB, pre=pre, post=post, ple_norms=ple_norms),
        grid=(n // tm,),
        in_specs=in_specs,
        out_specs=pl.BlockSpec((tm, d), lambda i: (i, 0)),
        out_shape=jax.ShapeDtypeStruct((n, d), F32),
        scratch_shapes=[pltpu.VMEM((tm, d), F32)],
        compiler_params=_cparams("parallel"),
        name="ffn" if ple is None else "ffn_ple",
    )(*args)


def _ab_in_kernel(*refs, w, pre, n_heads, chained):
    x_ref, g_ref, w_ref = refs[:3]
    xa_ref, ga_ref, q_ref, k_ref, v_ref, kh_ref, vh_ref = refs[5 if chained else 3:]
    hn = _rms(x_ref[...], g_ref[pre:pre + 1, :]).astype(BF16)
    for idx, o_ref in enumerate((xa_ref, ga_ref, q_ref)):
        o_ref[...] = _dot(hn, w_ref[:, idx * w:(idx + 1) * w])
    tm = x_ref.shape[0]
    hd = w // n_heads
    for idx, o_ref, oh_ref in ((3, k_ref, kh_ref), (4, v_ref, vh_ref)):
        y = _dot(hn, w_ref[:, idx * w:(idx + 1) * w])
        o_ref[...] = y
        for h in range(n_heads):
            oh_ref[pl.ds(h, tm, stride=n_heads), :] = y[:, h * hd:(h + 1) * hd]


def _ab_in(x, norm_g, w_in, layer, j, pre, n_layers, heads_prev=None):
    n, d = x.shape
    w = w_in.shape[-1] // 5
    n_heads = w // HEAD_B
    tm = _row_tile(n, 512)
    out = jax.ShapeDtypeStruct((n, w), F32)
    out_h = jax.ShapeDtypeStruct((n_layers, n * n_heads, HEAD_B), F32)
    ospec = pl.BlockSpec((tm, w), lambda i: (i, 0))
    hspec = pl.BlockSpec((None, tm * n_heads, HEAD_B), lambda i: (j, i, 0))
    in_specs = [pl.BlockSpec((tm, d), lambda i: (i, 0)), _layer_spec(norm_g, layer), _layer_spec(w_in, j)]
    args = [x, norm_g, w_in]
    aliases = {}
    if heads_prev is not None:
        in_specs += [pl.BlockSpec(memory_space=pl.ANY)] * 2
        args += list(heads_prev)
        aliases = {3: 5, 4: 6}
    return pl.pallas_call(
        functools.partial(_ab_in_kernel, w=w, pre=pre, n_heads=n_heads, chained=heads_prev is not None),
        grid=(n // tm,),
        in_specs=in_specs,
        out_specs=[ospec] * 5 + [hspec] * 2,
        out_shape=[out] * 5 + [out_h] * 2,
        input_output_aliases=aliases,
        compiler_params=_cparams("parallel"),
        name="ab_in",
    )(*args)


def _lru_kernel(xa_ref, ga_ref, buf_ref, h0_ref, cw_ref, cb_ref, wr_ref, br_ref, wi_ref, bi_ref, lam_ref,
                oa_ref, hl_ref, nb_ref, hc_ref, xc_ref, *, t):
    l = pl.program_id(1)

    @pl.when(l == 0)
    def _():
        hc_ref[...] = h0_ref[...]
        xc_ref[...] = jnp.zeros_like(xc_ref)
        xc_ref[SUBLANES - (CONV_W - 1):, :] = buf_ref[...]

    x = xa_ref[0]
    prev = xc_ref[...]
    row8 = lax.broadcasted_iota(jnp.int32, (SUBLANES, 1), 0)
    cw = cw_ref[...]
    y = cb_ref[...] + x * cw[CONV_W - 1:CONV_W]
    for k in range(1, CONV_W):
        xs = pltpu.roll(x, k, 0)
        head = jnp.where(row8 < k, pltpu.roll(prev, k, 0), xs[:SUBLANES])
        xs = jnp.concatenate([head, xs[SUBLANES:]], axis=0)
        y = y + xs * cw[CONV_W - 1 - k:CONV_W - k]
    xc_ref[...] = x[t - SUBLANES:]

    yb = y.astype(BF16)
    rg = _sigmoid(_dot(yb, wr_ref[...]) + br_ref[...])
    ig = _sigmoid(_dot(yb, wi_ref[...]) + bi_ref[...])
    log_a = (-LRU_C) * rg * _softplus(-lam_ref[...])
    a = jnp.exp(log_a)
    b = jnp.sqrt(1.0 - jnp.exp(2.0 * log_a)) * ig * y

    row = lax.broadcasted_iota(jnp.int32, (t, 1), 0)
    d = 1
    while d < t:
        keep = row >= d
        a_sh = jnp.where(keep, pltpu.roll(a, d, 0), 1.0)
        b_sh = jnp.where(keep, pltpu.roll(b, d, 0), 0.0)
        b = a * b_sh + b
        a = a * a_sh
        d *= 2
    h = a * hc_ref[...] + b
    hc_ref[...] = h[t - 1:t]
    oa_ref[0] = _gelu_tanh(ga_ref[0]) * h

    @pl.when(l == pl.num_programs(1) - 1)
    def _():
        hl_ref[0] = h[t - 1:t]
        nb_ref[0] = xa_ref[0, t - (CONV_W - 1):t, :]


def _lru(xa, ga, buf, h0, lp, j, js):
    bsz, length, w = xa.shape
    t = _row_tile(length, 256)
    seq = pl.BlockSpec((1, t, w), lambda b, l: (b, l, 0))
    par = lambda a: _layer_spec(a, j)
    return pl.pallas_call(
        functools.partial(_lru_kernel, t=t),
        grid=(bsz, length // t),
        in_specs=[
            seq, seq,
            pl.BlockSpec((None, None, CONV_W - 1, w), lambda b, l: (js, b, 0, 0)),
            pl.BlockSpec((None, None, 1, w), lambda b, l: (js, b, 0, 0)),
            par(lp["conv_w"]), par(lp["conv_b"]), par(lp["w_r"]), par(lp["b_r"]),
            par(lp["w_i"]), par(lp["b_i"]), par(lp["lam"]),
        ],
        out_specs=[
            seq,
            pl.BlockSpec((1, 1, w), lambda b, l: (b, 0, 0)),
            pl.BlockSpec((1, CONV_W - 1, w), lambda b, l: (b, 0, 0)),
        ],
        out_shape=[
            jax.ShapeDtypeStruct((bsz, length, w), F32),
            jax.ShapeDtypeStruct((bsz, 1, w), F32),
            jax.ShapeDtypeStruct((bsz, CONV_W - 1, w), F32),
        ],
        scratch_shapes=[pltpu.VMEM((1, w), F32), pltpu.VMEM((SUBLANES, w), F32)],
        compiler_params=_cparams("parallel", "arbitrary"),
        name="lru",
    )(xa, ga, buf, h0, lp["conv_w"], lp["conv_b"], lp["w_r"], lp["b_r"], lp["w_i"], lp["b_i"], lp["lam"])


def _sb_step(qh, k_rows, v_rows, tri, cs, accs, mask, low):
    n_pairs = len(qh)
    tq = accs[0].shape[0]
    lanes = lambda p: slice(p * HEAD_PAIR, (p + 1) * HEAD_PAIR)
    zs = [_dot_nt(qh[p], k_rows[:, lanes(p)].astype(BF16)) for p in range(n_pairs)]
    ls = [-_softplus(z) for z in zs]
    if mask is not None:
        ls = [jnp.where(mask, l, 0.0) for l in ls]
    his = [l.astype(BF16) for l in ls]
    los = [(l - hi.astype(F32)).astype(BF16) for l, hi in zip(ls, his)]
    sums = _dot(jnp.concatenate(his + los, axis=0), tri)
    m = n_pairs * 2 * tq
    cs_out, accs_out = [], []
    for p in range(n_pairs):
        incl = sums[p * 2 * tq:(p + 1) * 2 * tq] + sums[m + p * 2 * tq:m + (p + 1) * 2 * tq]
        e = jnp.exp(zs[p] + cs[p] + incl)
        if mask is not None:
            e = jnp.where(mask, e, 0.0)
        e = e.astype(BF16)
        e2 = jnp.concatenate([e[:tq], e[tq:]], axis=1)
        vp = v_rows[:, lanes(p)]
        v2 = jnp.concatenate([jnp.where(low, vp, 0.0), jnp.where(low, 0.0, vp)], axis=0).astype(BF16)
        accs_out.append(accs[p] + _dot(e2, v2))
        cs_out.append(cs[p] + incl[:, 0:1])
    return tuple(cs_out), tuple(accs_out)


def _tri(n):
    r = lax.broadcasted_iota(jnp.int32, (n, n), 0)
    c = lax.broadcasted_iota(jnp.int32, (n, n), 1)
    return jnp.where(r >= c, 1.0, 0.0).astype(BF16)


def _sb_kernel(*refs, tq, tkp, n_past, n_pairs):
    if n_past:
        q_ref, kn_ref, vn_ref, kp_ref, vp_ref, o_ref = refs
    else:
        q_ref, kn_ref, vn_ref, o_ref = refs
    i = pl.program_id(1)
    low = lax.broadcasted_iota(jnp.int32, (1, HEAD_PAIR), 1) < HEAD_B
    r2 = lax.broadcasted_iota(jnp.int32, (2 * tq, tq), 0)
    causal = lax.broadcasted_iota(jnp.int32, (2 * tq, tq), 1) < jnp.where(r2 >= tq, r2 - tq, r2)
    tri_n = _tri(tq)

    q = q_ref[0] * (HEAD_B ** -0.5)
    qh = []
    for p in range(n_pairs):
        qp = q[:, p * HEAD_PAIR:(p + 1) * HEAD_PAIR]
        qh.append(jnp.concatenate([jnp.where(low, qp, 0.0), jnp.where(low, 0.0, qp)], axis=0).astype(BF16))

    def step(k_rows, v_rows, tri, cs, accs, mask):
        return _sb_step(qh, k_rows, v_rows, tri, cs, accs, mask, low)

    def alive(cs):
        m = functools.reduce(jnp.maximum, cs)
        return (jnp.max(m) > SB_DEAD).astype(jnp.int32)

    def sweep(limit, fetch, tri, flag, cs, accs):
        def cond(st):
            return jnp.logical_and(st[0] < limit, st[1] > 0)

        def body(st):
            n, _, cs, accs = st
            k_rows, v_rows = fetch(n)
            cs, accs = step(k_rows, v_rows, tri, cs, accs, None)
            return n + 1, alive(cs), cs, accs

        _, flag, cs, accs = lax.while_loop(cond, body, (jnp.int32(0), flag, cs, accs))
        return flag, cs, accs

    def fetch_new(n):
        rows = pl.ds(pl.multiple_of((i - 1 - n) * tq, tq), tq)
        return kn_ref[0, rows, :], vn_ref[0, rows, :]

    def fetch_past(n):
        rows = pl.ds(pl.multiple_of((n_past - 1 - n) * tkp, tkp), tkp)
        return kp_ref[rows, :], vp_ref[rows, :]

    own = pl.ds(pl.multiple_of(i * tq, tq), tq)
    cs = tuple(jnp.zeros((2 * tq, 1), F32) for _ in range(n_pairs))
    accs = tuple(jnp.zeros((tq, HEAD_PAIR), F32) for _ in range(n_pairs))
    cs, accs = step(kn_ref[0, own, :], vn_ref[0, own, :], tri_n, cs, accs, causal)
    flag, cs, accs = sweep(i, fetch_new, tri_n, alive(cs), cs, accs)
    if n_past:
        flag, cs, accs = sweep(n_past, fetch_past, _tri(tkp), flag, cs, accs)
    for p in range(n_pairs):
        o_ref[0, :, p * HEAD_PAIR:(p + 1) * HEAD_PAIR] = accs[p]


def _stick_breaking(q, k_new, v_new, past=None):
    bsz, length, w = q.shape
    tq = _row_tile(length, 128)
    tkp = 128
    qspec = pl.BlockSpec((1, tq, w), lambda b, i: (b, i, 0))
    new_spec = pl.BlockSpec((1, length, w), lambda b, i: (b, 0, 0))
    in_specs = [qspec, new_spec, new_spec]
    args = [q, k_new, v_new]
    n_past = 0
    if past is not None:
        k_past, v_past, j = past
        plen = k_past.shape[2]
        n_past = plen // tkp
        past_spec = pl.BlockSpec((None, None, plen, w), lambda b, i: (j, b, 0, 0))
        in_specs += [past_spec, past_spec]
        args += [k_past, v_past]
    return pl.pallas_call(
        functools.partial(_sb_kernel, tq=tq, tkp=tkp, n_past=n_past, n_pairs=w // HEAD_PAIR),
        grid=(bsz, length // tq),
        in_specs=in_specs,
        out_specs=qspec,
        out_shape=jax.ShapeDtypeStruct((bsz, length, w), F32),
        compiler_params=_cparams("parallel", "arbitrary"),
        name="stick_breaking",
    )(*args)


def _ab_out_kernel(x_ref, oa_ref, ob_ref, g_ref, w_ref, o_ref, *, w, post):
    y = _dot(oa_ref[...].astype(BF16), w_ref[0:w, :]) + _dot(ob_ref[...].astype(BF16), w_ref[w:2 * w, :])
    o_ref[...] = x_ref[...] + _rms(y, g_ref[post:post + 1, :])


def _ab_out(x, oa, ob, norm_g, w_out, layer, j, post):
    n, d = x.shape
    w = oa.shape[1]
    tm = _row_tile(n, 1024)
    return pl.pallas_call(
        functools.partial(_ab_out_kernel, w=w, post=post),
        grid=(n // tm,),
        in_specs=[
            pl.BlockSpec((tm, d), lambda i: (i, 0)),
            pl.BlockSpec((tm, w), lambda i: (i, 0)),
            pl.BlockSpec((tm, w), lambda i: (i, 0)),
            _layer_spec(norm_g, layer),
            _layer_spec(w_out, j),
        ],
        out_specs=pl.BlockSpec((tm, d), lambda i: (i, 0)),
        out_shape=jax.ShapeDtypeStruct((n, d), F32),
        compiler_params=_cparams("parallel"),
        name="ab_out",
    )(x, oa, ob, norm_g, w_out)


def _ret_in_kernel(x_ref, g_ref, w_ref, cos_ref, sin_ref, qkv_ref, gate_ref, hn_ref, *, dk, pre, n_qkv):
    c = pl.program_id(1)

    @pl.when(c == 0)
    def _():
        hn_ref[...] = _rms(x_ref[...], g_ref[pre:pre + 1, :]).astype(BF16)

    y = _dot(hn_ref[...], w_ref[...])

    @pl.when(c >= n_qkv)
    def _():
        gate_ref[...] = y

    @pl.when(jnp.logical_and(c >= 2, c < n_qkv))
    def _():
        qkv_ref[...] = y.astype(BF16)

    @pl.when(c < 2)
    def _():
        scale = jnp.where(c == 0, 1.0, dk ** -0.5)
        cos = cos_ref[...] * scale
        sin = sin_ref[...] * scale
        half = dk // 2
        for h in range(y.shape[1] // dk):
            x1 = y[:, h * dk:h * dk + half]
            x2 = y[:, h * dk + half:(h + 1) * dk]
            qkv_ref[:, h * dk:h * dk + half] = (x1 * cos - x2 * sin).astype(BF16)
            qkv_ref[:, h * dk + half:(h + 1) * dk] = (x2 * cos + x1 * sin).astype(BF16)


def _ret_in(x, norm_g, w_in, cos, sin, dk, wv, layer, j, pre):
    n, d = x.shape
    cols = w_in.shape[-1]
    n_qkv = (cols - wv) // d
    tm = _row_tile(min(n, cos.shape[0]), 1024)
    nmod = cos.shape[0] // tm
    tab = pl.BlockSpec((tm, dk // 2), lambda i, c: (i % nmod, 0))
    return pl.pallas_call(
        functools.partial(_ret_in_kernel, dk=dk, pre=pre, n_qkv=n_qkv),
        grid=(n // tm, cols // d),
        in_specs=[
            pl.BlockSpec((tm, d), lambda i, c: (i, 0)),
            _layer_spec(norm_g, layer),
            pl.BlockSpec((None, d, d), lambda i, c: (j, 0, c)),
            tab, tab,
        ],
        out_specs=[
            pl.BlockSpec((tm, d), lambda i, c: (i, jnp.minimum(c, n_qkv - 1))),
            pl.BlockSpec((tm, d), lambda i, c: (i, jnp.maximum(c - n_qkv, 0))),
        ],
        out_shape=[jax.ShapeDtypeStruct((n, cols - wv), BF16), jax.ShapeDtypeStruct((n, wv), F32)],
        scratch_shapes=[pltpu.VMEM((tm, d), BF16)],
        compiler_params=_cparams("parallel", "arbitrary"),
        name="ret_in",
    )(x, norm_g, w_in, cos, sin)


def _ret_kernel(*refs, t, hc, dk, dv, has_state):
    if has_state:
        lg_ref, q_ref, k_ref, v_ref, r0_ref, o_ref, rl_ref, r_ref = refs
    else:
        lg_ref, q_ref, k_ref, v_ref, o_ref, rl_ref, r_ref = refs
    l = pl.program_id(1)

    @pl.when(l == 0)
    def _():
        r_ref[...] = r0_ref[...] if has_state else jnp.zeros_like(r_ref)

    ri = lax.broadcasted_iota(jnp.int32, (t, t), 0)
    ci = lax.broadcasted_iota(jnp.int32, (t, t), 1)
    dist = jnp.abs(ri - ci).astype(F32)
    visible = (ci >> CHUNK_BITS) <= (ri >> CHUNK_BITS)
    pos = lax.broadcasted_iota(jnp.int32, (t, 1), 0).astype(F32)
    heads = range(hc)
    lg = [lg_ref[h] for h in heads]
    q = [q_ref[0, :, h * dk:(h + 1) * dk] for h in heads]
    k = [k_ref[0, :, h * dk:(h + 1) * dk] for h in heads]
    v = [v_ref[0, :, h * dv:(h + 1) * dv] for h in heads]
    r = [r_ref[h] for h in heads]
    s = [_dot_nt(q[h], k[h]) for h in heads]
    cross = [_dot(q[h], r[h].astype(BF16)) for h in heads]
    kz = [(k[h].astype(F32) * jnp.exp(lg[h] * (t - 1.0 - pos))).astype(BF16) for h in heads]
    upd = [_dot_tn(kz[h], v[h]) for h in heads]
    for h in heads:
        sm = jnp.where(visible, s[h] * jnp.exp(lg[h] * dist), 0.0).astype(BF16)
        o_ref[0, :, h * dv:(h + 1) * dv] = _dot(sm, v[h]) + cross[h] * jnp.exp(lg[h] * (pos + 1.0))
        r_ref[h] = jnp.exp(jnp.full((1, 1), t, F32) * lg[h]) * r[h] + upd[h]

    @pl.when(l == pl.num_programs(1) - 1)
    def _():
        rl_ref[0] = r_ref[...]


def _retention(qkv, state, log_g, bsz, length, hc, dk, dv):
    t = _row_tile(length, 256)
    wq, wv = hc * dk, hc * dv
    in_specs = [
        pl.BlockSpec(memory_space=pltpu.SMEM),
        pl.BlockSpec((1, t, wq), lambda b, l: (b, l, 0)),
        pl.BlockSpec((1, t, wq), lambda b, l: (b, l, 1)),
        pl.BlockSpec((1, t, wv), lambda b, l: (b, l, 2 * wq // wv)),
    ]
    args = [log_g, qkv, qkv, qkv]
    if state is not None:
        states, j = state
        in_specs.append(pl.BlockSpec((None, None, hc, dk, dv), lambda b, l: (j, b, 0, 0, 0)))
        args.append(states)
    return pl.pallas_call(
        functools.partial(_ret_kernel, t=t, hc=hc, dk=dk, dv=dv, has_state=state is not None),
        grid=(bsz, length // t),
        in_specs=in_specs,
        out_specs=[
            pl.BlockSpec((1, t, wv), lambda b, l: (b, l, 0)),
            pl.BlockSpec((1, hc, dk, dv), lambda b, l: (b, 0, 0, 0)),
        ],
        out_shape=[
            jax.ShapeDtypeStruct((bsz, length, wv), F32),
            jax.ShapeDtypeStruct((bsz, hc, dk, dv), F32),
        ],
        scratch_shapes=[pltpu.VMEM((hc, dk, dv), F32)],
        compiler_params=_cparams("parallel", "arbitrary"),
        name="retention",
    )(*args)


def _ret_out_kernel(x_ref, o_ref_in, gate_ref, g_ref, w_ref, out_ref, *, hc, dv, post):
    y = None
    for h in range(hc):
        sl = slice(h * dv, (h + 1) * dv)
        o = o_ref_in[:, sl]
        mu = jnp.mean(o, axis=-1, keepdims=True)
        oc = o - mu
        on = oc * lax.rsqrt(jnp.mean(oc * oc, axis=-1, keepdims=True) + EPS)
        gt = gate_ref[:, sl]
        d = _dot((gt * _sigmoid(gt) * on).astype(BF16), w_ref[sl, :])
        y = d if y is None else y + d
    out_ref[...] = x_ref[...] + _rms(y, g_ref[post:post + 1, :])


def _ret_out(x, o, gate, norm_g, w_out, hc, dv, layer, j, post):
    n, d = x.shape
    wv = hc * dv
    tm = _row_tile(n, 512)
    return pl.pallas_call(
        functools.partial(_ret_out_kernel, hc=hc, dv=dv, post=post),
        grid=(n // tm,),
        in_specs=[
            pl.BlockSpec((tm, d), lambda i: (i, 0)),
            pl.BlockSpec((tm, wv), lambda i: (i, 0)),
            pl.BlockSpec((tm, wv), lambda i: (i, 0)),
            _layer_spec(norm_g, layer),
            _layer_spec(w_out, j),
        ],
        out_specs=pl.BlockSpec((tm, d), lambda i: (i, 0)),
        out_shape=jax.ShapeDtypeStruct((n, d), F32),
        compiler_params=_cparams("parallel"),
        name="ret_out",
    )(x, o, gate, norm_g, w_out)


def _block_diag(w):
    n, h, bi, bj = w.shape
    eye = jnp.eye(h, dtype=w.dtype)
    return (eye[None, :, None, :, None] * w[:, :, :, None, :]).reshape(n, h * bi, h * bj)


def _run_group(x, p, conv_state, lru_state, past_k, past_v, ret_state, pos0, wts):
    bsz, length, d = x.shape
    n = bsz * length
    depth = p.shape[0]
    x = x.reshape(n, d)
    p = p.reshape(depth, n, -1)
    norm_g = wts["norm_g"]
    hc, dk, dv = wts["hc"], wts["dk"], wts["dv"]
    w = wts["lru"]["lam"].shape[-1]
    pos = (pos0 + jnp.arange(length)).astype(F32)
    half = dk // 2
    freq = ROPE_BASE ** (-jnp.arange(half, dtype=F32) / half)
    ang = pos[:, None] * freq[None, :]
    reps = max(1, min(n, 1024) // length)
    cos = jnp.tile(jnp.cos(ang), (reps, 1))
    sin = jnp.tile(jnp.sin(ang), (reps, 1))
    fresh = conv_state is None
    if fresh:
        conv_state = jnp.zeros((1, bsz, CONV_W - 1, w), F32)
        lru_state = jnp.zeros((1, bsz, w), F32)
    lru_state = lru_state.reshape(-1, bsz, 1, w)
    if not fresh:
        plen = past_k.shape[2]
        past_k = past_k.reshape(-1, bsz, plen, w)
        past_v = past_v.reshape(-1, bsz, plen, w)
    seq = lambda a: a.reshape(bsz, length, -1)
    n_even = (depth + 1) // 2
    new_h, new_conv, new_ret, kv_heads = [], [], [], None
    for i in range(depth):
        j = i // 2
        js = 0 if fresh else j
        x = _ffn(x, norm_g, wts["ffn_w_gate"], wts["ffn_w_up"], wts["ffn_w_down"], i, 0, 0, 1)
        if i % 2 == 0:
            xa, ga, q, k, v, *kv_heads = _ab_in(x, norm_g, wts["ab_w_in"], i, j, 2, n_even, kv_heads)
            oa, h_last, nbuf = _lru(seq(xa), seq(ga), conv_state, lru_state, wts["lru"], j, js)
            ob = _stick_breaking(seq(q), seq(k), seq(v), None if fresh else (past_k, past_v, j))
            x = _ab_out(x, oa.reshape(n, w), ob.reshape(n, w), norm_g, wts["ab_w_out"], i, j, 3)
            new_h.append(h_last.reshape(bsz, w))
            new_conv.append(nbuf)
        else:
            qkv, gate = _ret_in(x, norm_g, wts["ret_w_in"], cos, sin, dk, hc * dv, i, j, 2)
            o, r_last = _retention(seq(qkv), None if fresh else (ret_state, j), wts["log_g"],
                                   bsz, length, hc, dk, dv)
            x = _ret_out(x, o.reshape(n, hc * dv), gate, norm_g, wts["ret_w_out"], hc, dv, i, j, 3)
            new_ret.append(r_last)
        x = _ffn(x, norm_g, wts["ffn_w_gate"], wts["ffn_w_up"], wts["ffn_w_down"], i, 1, 4, 5,
                 ple=(p, wts["ple_w_gate"], wts["ple_w_in"], 6, 7))
    new_k, new_v = (a.reshape(n_even, bsz, length, w // HEAD_B, HEAD_B) for a in kv_heads)
    return (x.reshape(bsz, length, d), jnp.stack(new_h), jnp.stack(new_conv), new_k, new_v, jnp.stack(new_ret))


def kernel(x_prompt, x_sample, p_prompt, p_sample, state_lru_h, state_conv, cache_sb_k, cache_sb_v, state_ret,
           norm_g, ffn_w_gate, ffn_w_up, ffn_w_down, ple_w_in, ple_w_gate,
           ab_w_in, ab_w_out, lru_conv_w, lru_conv_b, lru_w_r, lru_b_r, lru_w_i, lru_b_i, lru_lambda,
           ret_w_in, ret_w_out):
    assert norm_g.shape[1] == N_NORMS
    hc, dk, dv = state_ret.shape[2:]
    row = lambda a: a[:, None, :]
    wts = dict(
        hc=hc, dk=dk, dv=dv,
        norm_g=norm_g,
        ffn_w_gate=ffn_w_gate.astype(BF16), ffn_w_up=ffn_w_up.astype(BF16), ffn_w_down=ffn_w_down.astype(BF16),
        ple_w_in=ple_w_in.astype(BF16), ple_w_gate=ple_w_gate.astype(BF16),
        ab_w_in=ab_w_in.astype(BF16), ab_w_out=ab_w_out.astype(BF16),
        lru=dict(
            conv_w=lru_conv_w, conv_b=row(lru_conv_b),
            w_r=_block_diag(lru_w_r).astype(BF16), b_r=row(lru_b_r),
            w_i=_block_diag(lru_w_i).astype(BF16), b_i=row(lru_b_i),
            lam=row(lru_lambda),
        ),
        ret_w_in=ret_w_in.astype(BF16), ret_w_out=ret_w_out.astype(BF16),
        log_g=jnp.log(1.0 - 2.0 ** (-5.0 - jnp.arange(hc, dtype=F32))),
    )
    y_p, h_p, c_p, k_p, v_p, r_p = _run_group(x_prompt, p_prompt, None, None, None, None, None, 0, wts)
    y_s, h_s, c_s, k_s, v_s, r_s = _run_group(
        x_sample, p_sample, state_conv, state_lru_h, cache_sb_k, cache_sb_v, state_ret, cache_sb_k.shape[2], wts)
    return (y_p, y_s, h_p, c_p, k_p, v_p, r_p, h_s, c_s, k_s, v_s, r_s)
```

```python
import functools

import jax
import jax.numpy as jnp
from jax import lax
from jax.experimental import pallas as pl
from jax.experimental.pallas import tpu as pltpu

F32 = jnp.float32
BF16 = jnp.bfloat16

EPS = 1e-6
LRU_C = 8.0
CONV_W = 4
CHUNK_BITS = 6
ROPE_BASE = 10000.0
HEAD_B = 64
HEAD_PAIR = 2 * HEAD_B
SB_DEAD = -40.0
SUBLANES = 8
FF_SUB = 256
VMEM_LIMIT_BYTES = 56 * 1024 * 1024
N_NORMS = 8


def _cparams(*sem):
    return pltpu.CompilerParams(dimension_semantics=sem, vmem_limit_bytes=VMEM_LIMIT_BYTES)


def _dot(a, b):
    return jnp.dot(a, b, preferred_element_type=F32)


def _dot_nt(a, b):
    return lax.dot_general(a, b, (((1,), (1,)), ((), ())), preferred_element_type=F32)


def _dot_tn(a, b):
    return lax.dot_general(a, b, (((0,), (0,)), ((), ())), preferred_element_type=F32)


def _rms(x, g):
    return x * lax.rsqrt(jnp.mean(x * x, axis=-1, keepdims=True) + EPS) * g


def _sigmoid(x):
    return 1.0 / (1.0 + jnp.exp(-x))


def _softplus(x):
    return jnp.maximum(x, 0.0) + jnp.log(1.0 + jnp.exp(-jnp.abs(x)))


def _gelu_tanh(x):
    return 0.5 * x * (1.0 + jnp.tanh(0.7978845608028654 * (x + 0.044715 * x * x * x)))


def _row_tile(n, pref):
    t = min(n, pref)
    while n % t:
        t //= 2
    return t


def _layer_spec(arr, layer, pipeline_mode=None):
    layer = layer if isinstance(layer, tuple) else (layer,)
    rest = arr.shape[len(layer):]
    idx = layer + (0,) * len(rest)
    return pl.BlockSpec((None,) * len(layer) + rest, lambda *_: idx, pipeline_mode=pipeline_mode)


def _ffn_kernel(*refs, n_sub, pre, post, ple_norms):
    if ple_norms is None:
        x_ref, g_ref, wg_ref, wu_ref, wd_ref, o_ref, acc_ref = refs
    else:
        x_ref, g_ref, wg_ref, wu_ref, wd_ref, p_ref, wgate_ref, win_ref, o_ref, acc_ref = refs
    xn = _rms(x_ref[...], g_ref[pre:pre + 1, :]).astype(BF16)
    for c in range(n_sub):
        sl = pl.ds(c * FF_SUB, FF_SUB)
        h = _dot(xn, wg_ref[:, sl])
        u = _dot(xn, wu_ref[:, sl])
        a = (h * _sigmoid(h) * u).astype(BF16)
        d = _dot(a, wd_ref[sl, :])
        if c == 0:
            acc_ref[...] = d
        else:
            acc_ref[...] += d
    x = x_ref[...] + 0.5 * _rms(acc_ref[...], g_ref[post:post + 1, :])
    if ple_norms is not None:
        pre2, post2 = ple_norms
        gate = _sigmoid(_dot(_rms(x, g_ref[pre2:pre2 + 1, :]).astype(BF16), wgate_ref[...]))
        e = _dot(p_ref[...].astype(BF16), win_ref[...])
        x = x + _rms(gate * e, g_ref[post2:post2 + 1, :])
    o_ref[...] = x


def _ffn(x, norm_g, wg, wu, wd, layer, which, pre, post, ple=None):
    n, d = x.shape
    f = wg.shape[-1]
    tm = _row_tile(n, 1024)
    once = pl.Buffered(1)
    in_specs = [
        pl.BlockSpec((tm, d), lambda i: (i, 0)),
        _layer_spec(norm_g, layer),
        _layer_spec(wg, (layer, which), once),
        _layer_spec(wu, (layer, which), once),
        _layer_spec(wd, (layer, which), once),
    ]
    args = [x, norm_g, wg, wu, wd]
    ple_norms = None
    if ple is not None:
        p, wgate, win, pre2, post2 = ple
        ple_norms = (pre2, post2)
        in_specs += [
            pl.BlockSpec((None, tm, p.shape[-1]), lambda i: (layer, i, 0)),
            _layer_spec(wgate, layer, once),
            _layer_spec(win, layer, once),
        ]
        args += [p, wgate, win]
    return pl.pallas_call(
        functools.partial(_ffn_kernel, n_sub=f // FF_SUB, pre=pre, post=post, ple_norms=ple_norms),
        grid=(n // tm,),
        in_specs=in_specs,
        out_specs=pl.BlockSpec((tm, d), lambda i: (i, 0)),
        out_shape=jax.ShapeDtypeStruct((n, d), F32),
        scratch_shapes=[pltpu.VMEM((tm, d), F32)],
        compiler_params=_cparams("parallel"),
        name="ffn" if ple is None else "ffn_ple",
    )(*args)


def _ab_in_kernel(*refs, w, pre, n_heads, chained):
    x_ref, g_ref, w_ref = refs[:3]
    xa_ref, ga_ref, q_ref, k_ref, v_ref, kh_ref, vh_ref = refs[5 if chained else 3:]
    hn = _rms(x_ref[...], g_ref[pre:pre + 1, :]).astype(BF16)
    for idx, o_ref in enumerate((xa_ref, ga_ref)):
        o_ref[...] = _dot(hn, w_ref[:, idx * w:(idx + 1) * w])
    hd = w // n_heads
    q_ref[...] = (_dot(hn, w_ref[:, 2 * w:3 * w]) * hd ** -0.5).astype(BF16)
    tm = x_ref.shape[0]
    for idx, o_ref, oh_ref in ((3, k_ref, kh_ref), (4, v_ref, vh_ref)):
        y = _dot(hn, w_ref[:, idx * w:(idx + 1) * w])
        o_ref[...] = y.astype(BF16)
        for h in range(n_heads):
            oh_ref[pl.ds(h, tm, stride=n_heads), :] = y[:, h * hd:(h + 1) * hd]


def _ab_in(x, norm_g, w_in, layer, j, pre, n_layers, heads_prev=None):
    n, d = x.shape
    w = w_in.shape[-1] // 5
    n_heads = w // HEAD_B
    tm = _row_tile(n, 512)
    out = jax.ShapeDtypeStruct((n, w), F32)
    out_b = jax.ShapeDtypeStruct((n, w), BF16)
    out_h = jax.ShapeDtypeStruct((n_layers, n * n_heads, HEAD_B), F32)
    ospec = pl.BlockSpec((tm, w), lambda i: (i, 0))
    hspec = pl.BlockSpec((None, tm * n_heads, HEAD_B), lambda i: (j, i, 0))
    in_specs = [pl.BlockSpec((tm, d), lambda i: (i, 0)), _layer_spec(norm_g, layer), _layer_spec(w_in, j)]
    args = [x, norm_g, w_in]
    aliases = {}
    if heads_prev is not None:
        in_specs += [pl.BlockSpec(memory_space=pl.ANY)] * 2
        args += list(heads_prev)
        aliases = {3: 5, 4: 6}
    return pl.pallas_call(
        functools.partial(_ab_in_kernel, w=w, pre=pre, n_heads=n_heads, chained=heads_prev is not None),
        grid=(n // tm,),
        in_specs=in_specs,
        out_specs=[ospec] * 5 + [hspec] * 2,
        out_shape=[out] * 2 + [out_b] * 3 + [out_h] * 2,
        input_output_aliases=aliases,
        compiler_params=_cparams("parallel"),
        name="ab_in",
    )(*args)


def _lru_kernel(xa_ref, ga_ref, buf_ref, h0_ref, cw_ref, cb_ref, wr_ref, br_ref, wi_ref, bi_ref, lam_ref,
                oa_ref, hl_ref, nb_ref, hc_ref, xc_ref, *, t):
    l = pl.program_id(1)

    @pl.when(l == 0)
    def _():
        hc_ref[...] = h0_ref[...]
        xc_ref[...] = jnp.zeros_like(xc_ref)
        xc_ref[SUBLANES - (CONV_W - 1):, :] = buf_ref[...]

    x = xa_ref[0]
    prev = xc_ref[...]
    row8 = lax.broadcasted_iota(jnp.int32, (SUBLANES, 1), 0)
    cw = cw_ref[...]
    y = cb_ref[...] + x * cw[CONV_W - 1:CONV_W]
    for k in range(1, CONV_W):
        xs = pltpu.roll(x, k, 0)
        head = jnp.where(row8 < k, pltpu.roll(prev, k, 0), xs[:SUBLANES])
        xs = jnp.concatenate([head, xs[SUBLANES:]], axis=0)
        y = y + xs * cw[CONV_W - 1 - k:CONV_W - k]
    xc_ref[...] = x[t - SUBLANES:]

    yb = y.astype(BF16)
    rg = _sigmoid(_dot(yb, wr_ref[...]) + br_ref[...])
    ig = _sigmoid(_dot(yb, wi_ref[...]) + bi_ref[...])
    log_a = (-LRU_C) * rg * _softplus(-lam_ref[...])
    a = jnp.exp(log_a)
    b = jnp.sqrt(1.0 - jnp.exp(2.0 * log_a)) * ig * y

    row = lax.broadcasted_iota(jnp.int32, (t, 1), 0)
    d = 1
    while d < t:
        keep = row >= d
        a_sh = jnp.where(keep, pltpu.roll(a, d, 0), 1.0)
        b_sh = jnp.where(keep, pltpu.roll(b, d, 0), 0.0)
        b = a * b_sh + b
        a = a * a_sh
        d *= 2
    h = a * hc_ref[...] + b
    hc_ref[...] = h[t - 1:t]
    oa_ref[0] = (_gelu_tanh(ga_ref[0]) * h).astype(BF16)

    @pl.when(l == pl.num_programs(1) - 1)
    def _():
        hl_ref[0] = h[t - 1:t]
        nb_ref[0] = xa_ref[0, t - (CONV_W - 1):t, :]


def _lru(xa, ga, buf, h0, lp, j, js):
    bsz, length, w = xa.shape
    t = _row_tile(length, 256)
    seq = pl.BlockSpec((1, t, w), lambda b, l: (b, l, 0))
    par = lambda a: _layer_spec(a, j)
    return pl.pallas_call(
        functools.partial(_lru_kernel, t=t),
        grid=(bsz, length // t),
        in_specs=[
            seq, seq,
            pl.BlockSpec((None, None, CONV_W - 1, w), lambda b, l: (js, b, 0, 0)),
            pl.BlockSpec((None, None, 1, w), lambda b, l: (js, b, 0, 0)),
            par(lp["conv_w"]), par(lp["conv_b"]), par(lp["w_r"]), par(lp["b_r"]),
            par(lp["w_i"]), par(lp["b_i"]), par(lp["lam"]),
        ],
        out_specs=[
            seq,
            pl.BlockSpec((1, 1, w), lambda b, l: (b, 0, 0)),
            pl.BlockSpec((1, CONV_W - 1, w), lambda b, l: (b, 0, 0)),
        ],
        out_shape=[
            jax.ShapeDtypeStruct((bsz, length, w), BF16),
            jax.ShapeDtypeStruct((bsz, 1, w), F32),
            jax.ShapeDtypeStruct((bsz, CONV_W - 1, w), F32),
        ],
        scratch_shapes=[pltpu.VMEM((1, w), F32), pltpu.VMEM((SUBLANES, w), F32)],
        compiler_params=_cparams("parallel", "arbitrary"),
        name="lru",
    )(xa, ga, buf, h0, lp["conv_w"], lp["conv_b"], lp["w_r"], lp["b_r"], lp["w_i"], lp["b_i"], lp["lam"])


def _sb_step(qh, k_blk, v_blk, tri, cs, accs, mask, transposed):
    n_pairs = len(qh)
    tq = accs[0].shape[0]
    pair = lambda p: slice(p * HEAD_PAIR, (p + 1) * HEAD_PAIR)
    if transposed:
        zs = [_dot(qh[p], k_blk[pair(p), :].astype(BF16)) for p in range(n_pairs)]
        low = lax.broadcasted_iota(jnp.int32, (HEAD_PAIR, 1), 0) < HEAD_B
    else:
        zs = [_dot_nt(qh[p], k_blk[:, pair(p)].astype(BF16)) for p in range(n_pairs)]
        low = lax.broadcasted_iota(jnp.int32, (1, HEAD_PAIR), 1) < HEAD_B
    ls = [-_softplus(z) for z in zs]
    if mask is not None:
        ls = [jnp.where(mask, l, 0.0) for l in ls]
    his = [l.astype(BF16) for l in ls]
    los = [(l - hi.astype(F32)).astype(BF16) for l, hi in zip(ls, his)]
    sums = _dot(jnp.concatenate(his + los, axis=0), tri)
    m = n_pairs * 2 * tq
    zero = jnp.zeros((), BF16)
    cs_out, accs_out = [], []
    for p in range(n_pairs):
        incl = sums[p * 2 * tq:(p + 1) * 2 * tq] + sums[m + p * 2 * tq:m + (p + 1) * 2 * tq]
        e = jnp.exp(zs[p] + cs[p] + incl)
        if mask is not None:
            e = jnp.where(mask, e, 0.0)
        e = e.astype(BF16)
        e2 = jnp.concatenate([e[:tq], e[tq:]], axis=1)
        if transposed:
            vp = v_blk[pair(p), :].astype(BF16)
            v2 = jnp.concatenate([jnp.where(low, vp, zero), jnp.where(low, zero, vp)], axis=1)
            accs_out.append(accs[p] + _dot_nt(e2, v2))
        else:
            vp = v_blk[:, pair(p)].astype(BF16)
            v2 = jnp.concatenate([jnp.where(low, vp, zero), jnp.where(low, zero, vp)], axis=0)
            accs_out.append(accs[p] + _dot(e2, v2))
        cs_out.append(cs[p] + incl[:, 0:1])
    return tuple(cs_out), tuple(accs_out)


def _tri(n):
    r = lax.broadcasted_iota(jnp.int32, (n, n), 0)
    c = lax.broadcasted_iota(jnp.int32, (n, n), 1)
    return jnp.where(r >= c, 1.0, 0.0).astype(BF16)


def _sb_kernel(*refs, tq, tkp, n_past, n_pairs):
    if n_past:
        q_ref, kn_ref, vn_ref, kp_ref, vp_ref, o_ref = refs
    else:
        q_ref, kn_ref, vn_ref, o_ref = refs
    i = pl.program_id(1)
    low = lax.broadcasted_iota(jnp.int32, (1, HEAD_PAIR), 1) < HEAD_B
    r2 = lax.broadcasted_iota(jnp.int32, (2 * tq, tq), 0)
    causal = lax.broadcasted_iota(jnp.int32, (2 * tq, tq), 1) < jnp.where(r2 >= tq, r2 - tq, r2)
    tri_n = _tri(tq)

    q = q_ref[0]
    zero = jnp.zeros((), q.dtype)
    qh = []
    for p in range(n_pairs):
        qp = q[:, p * HEAD_PAIR:(p + 1) * HEAD_PAIR]
        qh.append(jnp.concatenate([jnp.where(low, qp, zero), jnp.where(low, zero, qp)], axis=0))

    def step(k_blk, v_blk, tri, cs, accs, mask, transposed=False):
        return _sb_step(qh, k_blk, v_blk, tri, cs, accs, mask, transposed)

    def alive(cs):
        m = functools.reduce(jnp.maximum, cs)
        return (jnp.max(m) > SB_DEAD).astype(jnp.int32)

    def sweep(limit, fetch, tri, flag, cs, accs, transposed=False):
        def cond(st):
            return jnp.logical_and(st[0] < limit, st[1] > 0)

        def body(st):
            n, _, cs, accs = st
            k_blk, v_blk = fetch(n)
            cs, accs = step(k_blk, v_blk, tri, cs, accs, None, transposed)
            return n + 1, alive(cs), cs, accs

        _, flag, cs, accs = lax.while_loop(cond, body, (jnp.int32(0), flag, cs, accs))
        return flag, cs, accs

    def fetch_new(n):
        rows = pl.ds(pl.multiple_of((i - 1 - n) * tq, tq), tq)
        return kn_ref[0, rows, :], vn_ref[0, rows, :]

    def fetch_past(n):
        cols = pl.ds(pl.multiple_of((n_past - 1 - n) * tkp, tkp), tkp)
        return kp_ref[:, cols], vp_ref[:, cols]

    own = pl.ds(pl.multiple_of(i * tq, tq), tq)
    cs = tuple(jnp.zeros((2 * tq, 1), F32) for _ in range(n_pairs))
    accs = tuple(jnp.zeros((tq, HEAD_PAIR), F32) for _ in range(n_pairs))
    cs, accs = step(kn_ref[0, own, :], vn_ref[0, own, :], tri_n, cs, accs, causal)
    flag, cs, accs = sweep(i, fetch_new, tri_n, alive(cs), cs, accs)
    if n_past:
        flag, cs, accs = sweep(n_past, fetch_past, _tri(tkp), flag, cs, accs, transposed=True)
    for p in range(n_pairs):
        o_ref[0, :, p * HEAD_PAIR:(p + 1) * HEAD_PAIR] = accs[p].astype(BF16)


def _stick_breaking(q, k_new, v_new, past=None):
    bsz, length, w = q.shape
    tq = _row_tile(length, 128)
    tkp = 128
    qspec = pl.BlockSpec((1, tq, w), lambda b, i: (b, i, 0))
    new_spec = pl.BlockSpec((1, length, w), lambda b, i: (b, 0, 0))
    in_specs = [qspec, new_spec, new_spec]
    args = [q, k_new, v_new]
    n_past = 0
    if past is not None:
        k_past, v_past, j = past
        plen = k_past.shape[3]
        n_past = plen // tkp
        past_spec = pl.BlockSpec((None, None, w, plen), lambda b, i: (j, b, 0, 0))
        in_specs += [past_spec, past_spec]
        args += [k_past, v_past]
    return pl.pallas_call(
        functools.partial(_sb_kernel, tq=tq, tkp=tkp, n_past=n_past, n_pairs=w // HEAD_PAIR),
        grid=(bsz, length // tq),
        in_specs=in_specs,
        out_specs=qspec,
        out_shape=jax.ShapeDtypeStruct((bsz, length, w), BF16),
        compiler_params=_cparams("parallel", "arbitrary"),
        name="stick_breaking",
    )(*args)


def _ab_out_kernel(x_ref, oa_ref, ob_ref, g_ref, w_ref, o_ref, *, w, post):
    y = _dot(oa_ref[...], w_ref[0:w, :]) + _dot(ob_ref[...], w_ref[w:2 * w, :])
    o_ref[...] = x_ref[...] + _rms(y, g_ref[post:post + 1, :])


def _ab_out(x, oa, ob, norm_g, w_out, layer, j, post):
    n, d = x.shape
    w = oa.shape[1]
    tm = _row_tile(n, 1024)
    return pl.pallas_call(
        functools.partial(_ab_out_kernel, w=w, post=post),
        grid=(n // tm,),
        in_specs=[
            pl.BlockSpec((tm, d), lambda i: (i, 0)),
            pl.BlockSpec((tm, w), lambda i: (i, 0)),
            pl.BlockSpec((tm, w), lambda i: (i, 0)),
            _layer_spec(norm_g, layer),
            _layer_spec(w_out, j),
        ],
        out_specs=pl.BlockSpec((tm, d), lambda i: (i, 0)),
        out_shape=jax.ShapeDtypeStruct((n, d), F32),
        compiler_params=_cparams("parallel"),
        name="ab_out",
    )(x, oa, ob, norm_g, w_out)


def _ret_in_kernel(x_ref, g_ref, w_ref, cos_ref, sin_ref, qkv_ref, gate_ref, *, dk, hc, pre):
    hn = _rms(x_ref[...], g_ref[pre:pre + 1, :]).astype(BF16)
    wq = qkv_ref.shape[1]
    half = dk // 2
    cos = cos_ref[...]
    sin = sin_ref[...]
    cos_k = cos * dk ** -0.5
    sin_k = sin * dk ** -0.5
    for blk in range(w_ref.shape[1] // dk):
        lo = blk * dk
        y = _dot(hn, w_ref[:, lo:lo + dk])
        if blk < 2 * hc:
            c, s = (cos, sin) if blk < hc else (cos_k, sin_k)
            x1 = y[:, :half]
            x2 = y[:, half:]
            qkv_ref[:, lo:lo + half] = (x1 * c - x2 * s).astype(BF16)
            qkv_ref[:, lo + half:lo + dk] = (x2 * c + x1 * s).astype(BF16)
        elif lo < wq:
            qkv_ref[:, lo:lo + dk] = y.astype(BF16)
        else:
            gate_ref[:, lo - wq:lo - wq + dk] = y.astype(BF16)


def _ret_in(x, norm_g, w_in, cos, sin, hc, dk, wv, layer, j, pre):
    n, d = x.shape
    cols = w_in.shape[-1]
    tm = _row_tile(min(n, cos.shape[0]), 512)
    nmod = cos.shape[0] // tm
    tab = pl.BlockSpec((tm, dk // 2), lambda i: (i % nmod, 0))
    return pl.pallas_call(
        functools.partial(_ret_in_kernel, dk=dk, hc=hc, pre=pre),
        grid=(n // tm,),
        in_specs=[
            pl.BlockSpec((tm, d), lambda i: (i, 0)),
            _layer_spec(norm_g, layer),
            _layer_spec(w_in, j, pl.Buffered(1)),
            tab, tab,
        ],
        out_specs=[
            pl.BlockSpec((tm, cols - wv), lambda i: (i, 0)),
            pl.BlockSpec((tm, wv), lambda i: (i, 0)),
        ],
        out_shape=[jax.ShapeDtypeStruct((n, cols - wv), BF16), jax.ShapeDtypeStruct((n, wv), BF16)],
        compiler_params=_cparams("parallel"),
        name="ret_in",
    )(x, norm_g, w_in, cos, sin)


def _ret_kernel(*refs, t, hc, dk, dv, has_state, chained):
    lg_ref, q_ref, k_ref, v_ref = refs[:4]
    r0_ref = refs[4] if has_state else None
    o_ref, rl_ref, r_ref = refs[4 + has_state + chained:]
    l = pl.program_id(1)

    @pl.when(l == 0)
    def _():
        r_ref[...] = r0_ref[...] if has_state else jnp.zeros_like(r_ref)

    ri = lax.broadcasted_iota(jnp.int32, (t, t), 0)
    ci = lax.broadcasted_iota(jnp.int32, (t, t), 1)
    dist = jnp.abs(ri - ci).astype(F32)
    visible = (ci >> CHUNK_BITS) <= (ri >> CHUNK_BITS)
    pos = lax.broadcasted_iota(jnp.int32, (t, 1), 0).astype(F32)
    heads = range(hc)
    lg = [lg_ref[h] for h in heads]
    q = [q_ref[0, :, h * dk:(h + 1) * dk] for h in heads]
    k = [k_ref[0, :, h * dk:(h + 1) * dk] for h in heads]
    v = [v_ref[0, :, h * dv:(h + 1) * dv] for h in heads]
    r = [r_ref[h] for h in heads]
    s = [_dot_nt(q[h], k[h]) for h in heads]
    cross = [_dot(q[h], r[h].astype(BF16)) for h in heads]
    kz = [(k[h].astype(F32) * jnp.exp(lg[h] * (t - 1.0 - pos))).astype(BF16) for h in heads]
    upd = [_dot_tn(kz[h], v[h]) for h in heads]
    for h in heads:
        sm = jnp.where(visible, s[h] * jnp.exp(lg[h] * dist), 0.0).astype(BF16)
        o = _dot(sm, v[h]) + cross[h] * jnp.exp(lg[h] * (pos + 1.0))
        o_ref[0, :, h * dv:(h + 1) * dv] = o.astype(BF16)
        r_ref[h] = jnp.exp(jnp.full((1, 1), t, F32) * lg[h]) * r[h] + upd[h]

    @pl.when(l == pl.num_programs(1) - 1)
    def _():
        rl_ref[...] = r_ref[...]


def _retention(qkv, state, log_g, bsz, length, hc, dk, dv, j, n_layers, finals_prev=None):
    t = _row_tile(length, 256)
    wq, wv = hc * dk, hc * dv
    in_specs = [
        pl.BlockSpec(memory_space=pltpu.SMEM),
        pl.BlockSpec((1, t, wq), lambda b, l: (b, l, 0)),
        pl.BlockSpec((1, t, wq), lambda b, l: (b, l, 1)),
        pl.BlockSpec((1, t, wv), lambda b, l: (b, l, 2 * wq // wv)),
    ]
    args = [log_g, qkv, qkv, qkv]
    state_spec = pl.BlockSpec((None, None, hc, dk, dv), lambda b, l: (j, b, 0, 0, 0))
    if state is not None:
        in_specs.append(state_spec)
        args.append(state)
    aliases = {}
    if finals_prev is not None:
        aliases = {len(args): 1}
        in_specs.append(pl.BlockSpec(memory_space=pl.ANY))
        args.append(finals_prev)
    return pl.pallas_call(
        functools.partial(_ret_kernel, t=t, hc=hc, dk=dk, dv=dv, has_state=state is not None,
                          chained=finals_prev is not None),
        grid=(bsz, length // t),
        in_specs=in_specs,
        out_specs=[pl.BlockSpec((1, t, wv), lambda b, l: (b, l, 0)), state_spec],
        out_shape=[
            jax.ShapeDtypeStruct((bsz, length, wv), BF16),
            jax.ShapeDtypeStruct((n_layers, bsz, hc, dk, dv), F32),
        ],
        input_output_aliases=aliases,
        scratch_shapes=[pltpu.VMEM((hc, dk, dv), F32)],
        compiler_params=_cparams("parallel", "arbitrary"),
        name="retention",
    )(*args)


def _ret_out_kernel(x_ref, o_ref_in, gate_ref, g_ref, w_ref, out_ref, *, hc, dv, post):
    y = None
    for h in range(hc):
        sl = slice(h * dv, (h + 1) * dv)
        o = o_ref_in[:, sl].astype(F32)
        mu = jnp.mean(o, axis=-1, keepdims=True)
        oc = o - mu
        on = oc * lax.rsqrt(jnp.mean(oc * oc, axis=-1, keepdims=True) + EPS)
        gt = gate_ref[:, sl].astype(F32)
        d = _dot((gt * _sigmoid(gt) * on).astype(BF16), w_ref[sl, :])
        y = d if y is None else y + d
    out_ref[...] = x_ref[...] + _rms(y, g_ref[post:post + 1, :])


def _ret_out(x, o, gate, norm_g, w_out, hc, dv, layer, j, post):
    n, d = x.shape
    wv = hc * dv
    tm = _row_tile(n, 512)
    return pl.pallas_call(
        functools.partial(_ret_out_kernel, hc=hc, dv=dv, post=post),
        grid=(n // tm,),
        in_specs=[
            pl.BlockSpec((tm, d), lambda i: (i, 0)),
            pl.BlockSpec((tm, wv), lambda i: (i, 0)),
            pl.BlockSpec((tm, wv), lambda i: (i, 0)),
            _layer_spec(norm_g, layer),
            _layer_spec(w_out, j),
        ],
        out_specs=pl.BlockSpec((tm, d), lambda i: (i, 0)),
        out_shape=jax.ShapeDtypeStruct((n, d), F32),
        compiler_params=_cparams("parallel"),
        name="ret_out",
    )(x, o, gate, norm_g, w_out)


def _block_diag(w):
    n, h, bi, bj = w.shape
    eye = jnp.eye(h, dtype=w.dtype)
    return (eye[None, :, None, :, None] * w[:, :, :, None, :]).reshape(n, h * bi, h * bj)


def _run_group(x, p, conv_state, lru_state, past_k, past_v, ret_state, pos0, wts):
    bsz, length, d = x.shape
    n = bsz * length
    depth = p.shape[0]
    x = x.reshape(n, d)
    p = p.reshape(depth, n, -1)
    norm_g = wts["norm_g"]
    hc, dk, dv = wts["hc"], wts["dk"], wts["dv"]
    w = wts["lru"]["lam"].shape[-1]
    pos = (pos0 + jnp.arange(length)).astype(F32)
    half = dk // 2
    freq = ROPE_BASE ** (-jnp.arange(half, dtype=F32) / half)
    ang = pos[:, None] * freq[None, :]
    reps = max(1, min(n, 1024) // length)
    cos = jnp.tile(jnp.cos(ang), (reps, 1))
    sin = jnp.tile(jnp.sin(ang), (reps, 1))
    fresh = conv_state is None
    if fresh:
        conv_state = jnp.zeros((1, bsz, CONV_W - 1, w), F32)
        lru_state = jnp.zeros((1, bsz, w), F32)
    lru_state = lru_state.reshape(-1, bsz, 1, w)
    if not fresh:
        plen = past_k.shape[2]
        past_k = past_k.transpose(0, 1, 3, 4, 2).reshape(-1, bsz, w, plen)
        past_v = past_v.transpose(0, 1, 3, 4, 2).reshape(-1, bsz, w, plen)
    seq = lambda a: a.reshape(bsz, length, -1)
    n_even = (depth + 1) // 2
    new_h, new_conv, new_ret, kv_heads = [], [], None, None
    for i in range(depth):
        j = i // 2
        js = 0 if fresh else j
        x = _ffn(x, norm_g, wts["ffn_w_gate"], wts["ffn_w_up"], wts["ffn_w_down"], i, 0, 0, 1)
        if i % 2 == 0:
            xa, ga, q, k, v, *kv_heads = _ab_in(x, norm_g, wts["ab_w_in"], i, j, 2, n_even, kv_heads)
            oa, h_last, nbuf = _lru(seq(xa), seq(ga), conv_state, lru_state, wts["lru"], j, js)
            ob = _stick_breaking(seq(q), seq(k), seq(v), None if fresh else (past_k, past_v, j))
            x = _ab_out(x, oa.reshape(n, w), ob.reshape(n, w), norm_g, wts["ab_w_out"], i, j, 3)
            new_h.append(h_last.reshape(bsz, w))
            new_conv.append(nbuf)
        else:
            qkv, gate = _ret_in(x, norm_g, wts["ret_w_in"], cos, sin, hc, dk, hc * dv, i, j, 2)
            o, new_ret = _retention(seq(qkv), ret_state, wts["log_g"], bsz, length, hc, dk, dv,
                                    j, depth // 2, new_ret)
            x = _ret_out(x, o.reshape(n, hc * dv), gate, norm_g, wts["ret_w_out"], hc, dv, i, j, 3)
        x = _ffn(x, norm_g, wts["ffn_w_gate"], wts["ffn_w_up"], wts["ffn_w_down"], i, 1, 4, 5,
                 ple=(p, wts["ple_w_gate"], wts["ple_w_in"], 6, 7))
    new_k, new_v = (a.reshape(n_even, bsz, length, w // HEAD_B, HEAD_B) for a in kv_heads)
    return (x.reshape(bsz, length, d), jnp.stack(new_h), jnp.stack(new_conv), new_k, new_v, new_ret)


def kernel(x_prompt, x_sample, p_prompt, p_sample, state_lru_h, state_conv, cache_sb_k, cache_sb_v, state_ret,
           norm_g, ffn_w_gate, ffn_w_up, ffn_w_down, ple_w_in, ple_w_gate,
           ab_w_in, ab_w_out, lru_conv_w, lru_conv_b, lru_w_r, lru_b_r, lru_w_i, lru_b_i, lru_lambda,
           ret_w_in, ret_w_out):
    assert norm_g.shape[1] == N_NORMS
    hc, dk, dv = state_ret.shape[2:]
    row = lambda a: a[:, None, :]
    wts = dict(
        hc=hc, dk=dk, dv=dv,
        norm_g=norm_g,
        ffn_w_gate=ffn_w_gate.astype(BF16), ffn_w_up=ffn_w_up.astype(BF16), ffn_w_down=ffn_w_down.astype(BF16),
        ple_w_in=ple_w_in.astype(BF16), ple_w_gate=ple_w_gate.astype(BF16),
        ab_w_in=ab_w_in.astype(BF16), ab_w_out=ab_w_out.astype(BF16),
        lru=dict(
            conv_w=lru_conv_w, conv_b=row(lru_conv_b),
            w_r=_block_diag(lru_w_r).astype(BF16), b_r=row(lru_b_r),
            w_i=_block_diag(lru_w_i).astype(BF16), b_i=row(lru_b_i),
            lam=row(lru_lambda),
        ),
        ret_w_in=ret_w_in.astype(BF16), ret_w_out=ret_w_out.astype(BF16),
        log_g=jnp.log(1.0 - 2.0 ** (-5.0 - jnp.arange(hc, dtype=F32))),
    )
    y_p, h_p, c_p, k_p, v_p, r_p = _run_group(x_prompt, p_prompt, None, None, None, None, None, 0, wts)
    y_s, h_s, c_s, k_s, v_s, r_s = _run_group(
        x_sample, p_sample, state_conv, state_lru_h, cache_sb_k, cache_sb_v, state_ret, cache_sb_k.shape[2], wts)
    return (y_p, y_s, h_p, c_p, k_p, v_p, r_p, h_s, c_s, k_s, v_s, r_s)
```

```python
import functools

import jax
import jax.numpy as jnp
from jax import lax
from jax.experimental import pallas as pl
from jax.experimental.pallas import tpu as pltpu

F32 = jnp.float32
BF16 = jnp.bfloat16

EPS = 1e-6
LRU_C = 8.0
CONV_W = 4
CHUNK_BITS = 6
ROPE_BASE = 10000.0
HEAD_B = 64
HEAD_PAIR = 2 * HEAD_B
SB_DEAD = -40.0
SUBLANES = 8
FF_SUB = 256
VMEM_LIMIT_BYTES = 56 * 1024 * 1024
N_NORMS = 8


def _cparams(*sem):
    return pltpu.CompilerParams(dimension_semantics=sem, vmem_limit_bytes=VMEM_LIMIT_BYTES)


def _dot(a, b):
    return jnp.dot(a, b, preferred_element_type=F32)


def _dot_nt(a, b):
    return lax.dot_general(a, b, (((1,), (1,)), ((), ())), preferred_element_type=F32)


def _dot_tn(a, b):
    return lax.dot_general(a, b, (((0,), (0,)), ((), ())), preferred_element_type=F32)


def _rms(x, g):
    return x * lax.rsqrt(jnp.mean(x * x, axis=-1, keepdims=True) + EPS) * g


def _sigmoid(x):
    return 1.0 / (1.0 + jnp.exp(-x))


def _softplus(x):
    return jnp.maximum(x, 0.0) + jnp.log(1.0 + jnp.exp(-jnp.abs(x)))


def _gelu_tanh(x):
    return 0.5 * x * (1.0 + jnp.tanh(0.7978845608028654 * (x + 0.044715 * x * x * x)))


def _row_tile(n, pref):
    t = min(n, pref)
    while n % t:
        t //= 2
    return t


def _layer_spec(arr, layer, pipeline_mode=None):
    layer = layer if isinstance(layer, tuple) else (layer,)
    rest = arr.shape[len(layer):]
    idx = layer + (0,) * len(rest)
    return pl.BlockSpec((None,) * len(layer) + rest, lambda *_: idx, pipeline_mode=pipeline_mode)


def _ffn_kernel(*refs, n_sub, n_steps, emit, pre, post, mix_post, ple_norms):
    refs = list(refs)
    x_ref, g_ref, wg_ref, wu_ref, wd_ref = refs[:5]
    n_out = 4 if emit else 1
    n_scratch = 1 if n_steps == 1 else 2
    extra = refs[5:len(refs) - n_out - n_scratch]
    o_ref = refs[len(refs) - n_out - n_scratch]
    bf_refs = refs[len(refs) - n_scratch - 3:len(refs) - n_scratch] if emit else (None, None, None)
    acc_ref = refs[len(refs) - n_scratch]
    xn_ref = refs[-1] if n_steps > 1 else None
    step = pl.program_id(1)

    def head():
        x = x_ref[...]
        if mix_post is not None:
            a_ref, b_ref, wm_ref = extra[:3]
            wa = a_ref.shape[1]
            y = _dot(a_ref[...], wm_ref[0:wa, :]) + _dot(b_ref[...], wm_ref[wa:, :])
            x = x + _rms(y, g_ref[mix_post:mix_post + 1, :])
        o_ref[...] = x
        xn = _rms(x, g_ref[pre:pre + 1, :]).astype(BF16)
        if xn_ref is not None:
            xn_ref[...] = xn
            acc_ref[...] = jnp.zeros_like(acc_ref)
        return xn

    def tail():
        x = o_ref[...] + 0.5 * _rms(acc_ref[...], g_ref[post:post + 1, :])
        if ple_norms is not None:
            p_ref, wgate_ref, win_ref = extra[-3:]
            pre2, post2 = ple_norms
            gate = _sigmoid(_dot(_rms(x, g_ref[pre2:pre2 + 1, :]).astype(BF16), wgate_ref[...]))
            e = _dot(p_ref[...].astype(BF16), win_ref[...])
            x = x + _rms(gate * e, g_ref[post2:post2 + 1, :])
        o_ref[...] = x

    def weights(w_ref, bf_ref, idx):
        w = w_ref[idx]
        if emit:
            w = w.astype(BF16)
            bf_ref[idx] = w
        return w

    if n_steps == 1:
        xn = head()
    else:
        @pl.when(step == 0)
        def _():
            head()

        xn = xn_ref[...]
    for c in range(n_sub):
        sl = pl.ds(c * FF_SUB, FF_SUB)
        h = _dot(xn, weights(wg_ref, bf_refs[0], (slice(None), sl)))
        u = _dot(xn, weights(wu_ref, bf_refs[1], (slice(None), sl)))
        a = (h * _sigmoid(h) * u).astype(BF16)
        d = _dot(a, weights(wd_ref, bf_refs[2], (sl, slice(None))))
        if c == 0 and n_steps == 1:
            acc_ref[...] = d
        else:
            acc_ref[...] += d
    if n_steps == 1:
        tail()
    else:
        @pl.when(step == n_steps - 1)
        def _():
            tail()


def _ffn(x, norm_g, wg, wu, wd, layer, which, pre, post, mix=None, ple=None, emit=False):
    n, d = x.shape
    f = wg.shape[-1]
    tm = _row_tile(n, 1024)
    tf = FF_SUB if emit else f
    n_steps = f // tf
    lead = (None,) * len(which)
    once = None if emit else pl.Buffered(1)
    rows = lambda width: pl.BlockSpec((tm, width), lambda i, s: (i, 0))
    in_specs = [
        rows(d),
        _layer_spec(norm_g, layer),
        pl.BlockSpec(lead + (d, tf), lambda i, s: which + (0, s), pipeline_mode=once),
        pl.BlockSpec(lead + (d, tf), lambda i, s: which + (0, s), pipeline_mode=once),
        pl.BlockSpec(lead + (tf, d), lambda i, s: which + (s, 0), pipeline_mode=once),
    ]
    args = [x, norm_g, wg, wu, wd]
    mix_post = ple_norms = None
    resident = pl.Buffered(1)
    if mix is not None:
        a, b, wm, j, mix_post = mix
        in_specs += [rows(a.shape[1]), rows(b.shape[1]), _layer_spec(wm, j, resident)]
        args += [a, b, wm]
    if ple is not None:
        p, wgate, win, pre2, post2 = ple
        ple_norms = (pre2, post2)
        in_specs += [
            pl.BlockSpec((None, tm, p.shape[-1]), lambda i, s: (layer, i, 0)),
            _layer_spec(wgate, layer, resident),
            _layer_spec(win, layer, resident),
        ]
        args += [p, wgate, win]
    out_specs = [rows(d)]
    out_shape = [jax.ShapeDtypeStruct((n, d), F32)]
    scratch = [pltpu.VMEM((tm, d), F32)]
    if emit:
        assert n == tm, "the weight copies are written once, by a single row tile"
        out_specs += [
            pl.BlockSpec((d, tf), lambda i, s: (0, s)),
            pl.BlockSpec((d, tf), lambda i, s: (0, s)),
            pl.BlockSpec((tf, d), lambda i, s: (s, 0)),
        ]
        out_shape += [jax.ShapeDtypeStruct((d, f), BF16)] * 2 + [jax.ShapeDtypeStruct((f, d), BF16)]
    if n_steps > 1:
        scratch.append(pltpu.VMEM((tm, d), BF16))
    out = pl.pallas_call(
        functools.partial(_ffn_kernel, n_sub=tf // FF_SUB, n_steps=n_steps, emit=emit, pre=pre, post=post,
                          mix_post=mix_post, ple_norms=ple_norms),
        grid=(n // tm, n_steps),
        in_specs=in_specs,
        out_specs=out_specs,
        out_shape=out_shape,
        scratch_shapes=scratch,
        compiler_params=_cparams("parallel", "arbitrary"),
        name="ffn" + ("_mix" if mix is not None else "") + ("_ple" if ple is not None else "")
        + ("_cast" if emit else ""),
    )(*args)
    return out if emit else out[0]


def _ab_in_kernel(*refs, w, pre, n_heads, chained):
    x_ref, g_ref, w_ref = refs[:3]
    xa_ref, ga_ref, q_ref, k_ref, v_ref, kh_ref, vh_ref = refs[5 if chained else 3:]
    hn = _rms(x_ref[...], g_ref[pre:pre + 1, :]).astype(BF16)
    for idx, o_ref in enumerate((xa_ref, ga_ref)):
        o_ref[...] = _dot(hn, w_ref[:, idx * w:(idx + 1) * w])
    hd = w // n_heads
    q_ref[...] = (_dot(hn, w_ref[:, 2 * w:3 * w]) * hd ** -0.5).astype(BF16)
    tm = x_ref.shape[0]
    for idx, o_ref, oh_ref in ((3, k_ref, kh_ref), (4, v_ref, vh_ref)):
        y = _dot(hn, w_ref[:, idx * w:(idx + 1) * w])
        o_ref[...] = y.astype(BF16)
        for h in range(n_heads):
            oh_ref[pl.ds(h, tm, stride=n_heads), :] = y[:, h * hd:(h + 1) * hd]


def _ab_in(x, norm_g, w_in, layer, j, pre, n_layers, heads_prev=None):
    n, d = x.shape
    w = w_in.shape[-1] // 5
    n_heads = w // HEAD_B
    tm = _row_tile(n, 512)
    out = jax.ShapeDtypeStruct((n, w), F32)
    out_b = jax.ShapeDtypeStruct((n, w), BF16)
    out_h = jax.ShapeDtypeStruct((n_layers, n * n_heads, HEAD_B), F32)
    ospec = pl.BlockSpec((tm, w), lambda i: (i, 0))
    hspec = pl.BlockSpec((None, tm * n_heads, HEAD_B), lambda i: (j, i, 0))
    in_specs = [pl.BlockSpec((tm, d), lambda i: (i, 0)), _layer_spec(norm_g, layer), _layer_spec(w_in, j)]
    args = [x, norm_g, w_in]
    aliases = {}
    if heads_prev is not None:
        in_specs += [pl.BlockSpec(memory_space=pl.ANY)] * 2
        args += list(heads_prev)
        aliases = {3: 5, 4: 6}
    return pl.pallas_call(
        functools.partial(_ab_in_kernel, w=w, pre=pre, n_heads=n_heads, chained=heads_prev is not None),
        grid=(n // tm,),
        in_specs=in_specs,
        out_specs=[ospec] * 5 + [hspec] * 2,
        out_shape=[out] * 2 + [out_b] * 3 + [out_h] * 2,
        input_output_aliases=aliases,
        compiler_params=_cparams("parallel"),
        name="ab_in",
    )(*args)


def _lru_kernel(xa_ref, ga_ref, buf_ref, h0_ref, cw_ref, cb_ref, wr_ref, br_ref, wi_ref, bi_ref, lam_ref,
                oa_ref, hl_ref, nb_ref, hc_ref, xc_ref, *, t):
    l = pl.program_id(1)

    @pl.when(l == 0)
    def _():
        hc_ref[...] = h0_ref[...]
        xc_ref[...] = jnp.zeros_like(xc_ref)
        xc_ref[SUBLANES - (CONV_W - 1):, :] = buf_ref[...]

    x = xa_ref[0]
    prev = xc_ref[...]
    row8 = lax.broadcasted_iota(jnp.int32, (SUBLANES, 1), 0)
    cw = cw_ref[...]
    y = cb_ref[...] + x * cw[CONV_W - 1:CONV_W]
    for k in range(1, CONV_W):
        xs = pltpu.roll(x, k, 0)
        head = jnp.where(row8 < k, pltpu.roll(prev, k, 0), xs[:SUBLANES])
        xs = jnp.concatenate([head, xs[SUBLANES:]], axis=0)
        y = y + xs * cw[CONV_W - 1 - k:CONV_W - k]
    xc_ref[...] = x[t - SUBLANES:]

    yb = y.astype(BF16)
    rg = _sigmoid(_dot(yb, wr_ref[...]) + br_ref[...])
    ig = _sigmoid(_dot(yb, wi_ref[...]) + bi_ref[...])
    log_a = (-LRU_C) * rg * _softplus(-lam_ref[...])
    a = jnp.exp(log_a)
    b = jnp.sqrt(1.0 - jnp.exp(2.0 * log_a)) * ig * y

    row = lax.broadcasted_iota(jnp.int32, (t, 1), 0)
    d = 1
    while d < t:
        keep = row >= d
        a_sh = jnp.where(keep, pltpu.roll(a, d, 0), 1.0)
        b_sh = jnp.where(keep, pltpu.roll(b, d, 0), 0.0)
        b = a * b_sh + b
        a = a * a_sh
        d *= 2
    h = a * hc_ref[...] + b
    hc_ref[...] = h[t - 1:t]
    oa_ref[0] = (_gelu_tanh(ga_ref[0]) * h).astype(BF16)

    @pl.when(l == pl.num_programs(1) - 1)
    def _():
        hl_ref[0] = h[t - 1:t]
        nb_ref[0] = xa_ref[0, t - (CONV_W - 1):t, :]


def _lru(xa, ga, buf, h0, lp, j, js):
    bsz, length, w = xa.shape
    t = _row_tile(length, 256)
    seq = pl.BlockSpec((1, t, w), lambda b, l: (b, l, 0))
    par = lambda a: _layer_spec(a, j)
    return pl.pallas_call(
        functools.partial(_lru_kernel, t=t),
        grid=(bsz, length // t),
        in_specs=[
            seq, seq,
            pl.BlockSpec((None, None, CONV_W - 1, w), lambda b, l: (js, b, 0, 0)),
            pl.BlockSpec((None, None, 1, w), lambda b, l: (js, b, 0, 0)),
            par(lp["conv_w"]), par(lp["conv_b"]), par(lp["w_r"]), par(lp["b_r"]),
            par(lp["w_i"]), par(lp["b_i"]), par(lp["lam"]),
        ],
        out_specs=[
            seq,
            pl.BlockSpec((1, 1, w), lambda b, l: (b, 0, 0)),
            pl.BlockSpec((1, CONV_W - 1, w), lambda b, l: (b, 0, 0)),
        ],
        out_shape=[
            jax.ShapeDtypeStruct((bsz, length, w), BF16),
            jax.ShapeDtypeStruct((bsz, 1, w), F32),
            jax.ShapeDtypeStruct((bsz, CONV_W - 1, w), F32),
        ],
        scratch_shapes=[pltpu.VMEM((1, w), F32), pltpu.VMEM((SUBLANES, w), F32)],
        compiler_params=_cparams("parallel", "arbitrary"),
        name="lru",
    )(xa, ga, buf, h0, lp["conv_w"], lp["conv_b"], lp["w_r"], lp["b_r"], lp["w_i"], lp["b_i"], lp["lam"])


def _sb_step(qh, k_blk, v_blk, tri, cs, accs, mask, transposed):
    n_pairs = len(qh)
    tq = accs[0].shape[0]
    pair = lambda p: slice(p * HEAD_PAIR, (p + 1) * HEAD_PAIR)
    if transposed:
        zs = [_dot(qh[p], k_blk[pair(p), :].astype(BF16)) for p in range(n_pairs)]
        low = lax.broadcasted_iota(jnp.int32, (HEAD_PAIR, 1), 0) < HEAD_B
    else:
        zs = [_dot_nt(qh[p], k_blk[:, pair(p)].astype(BF16)) for p in range(n_pairs)]
        low = lax.broadcasted_iota(jnp.int32, (1, HEAD_PAIR), 1) < HEAD_B
    ls = [-_softplus(z) for z in zs]
    if mask is not None:
        ls = [jnp.where(mask, l, 0.0) for l in ls]
    his = [l.astype(BF16) for l in ls]
    los = [(l - hi.astype(F32)).astype(BF16) for l, hi in zip(ls, his)]
    sums = _dot(jnp.concatenate(his + los, axis=0), tri)
    m = n_pairs * 2 * tq
    zero = jnp.zeros((), BF16)
    cs_out, accs_out = [], []
    for p in range(n_pairs):
        incl = sums[p * 2 * tq:(p + 1) * 2 * tq] + sums[m + p * 2 * tq:m + (p + 1) * 2 * tq]
        e = jnp.exp(zs[p] + cs[p] + incl)
        if mask is not None:
            e = jnp.where(mask, e, 0.0)
        e = e.astype(BF16)
        e2 = jnp.concatenate([e[:tq], e[tq:]], axis=1)
        if transposed:
            vp = v_blk[pair(p), :].astype(BF16)
            v2 = jnp.concatenate([jnp.where(low, vp, zero), jnp.where(low, zero, vp)], axis=1)
            accs_out.append(accs[p] + _dot_nt(e2, v2))
        else:
            vp = v_blk[:, pair(p)].astype(BF16)
            v2 = jnp.concatenate([jnp.where(low, vp, zero), jnp.where(low, zero, vp)], axis=0)
            accs_out.append(accs[p] + _dot(e2, v2))
        cs_out.append(cs[p] + incl[:, 0:1])
    return tuple(cs_out), tuple(accs_out)


def _tri(n):
    r = lax.broadcasted_iota(jnp.int32, (n, n), 0)
    c = lax.broadcasted_iota(jnp.int32, (n, n), 1)
    return jnp.where(r >= c, 1.0, 0.0).astype(BF16)


def _sb_kernel(*refs, tq, tkp, n_past, n_pairs):
    if n_past:
        q_ref, kn_ref, vn_ref, kp_ref, vp_ref, o_ref = refs
    else:
        q_ref, kn_ref, vn_ref, o_ref = refs
    i = pl.program_id(1)
    low = lax.broadcasted_iota(jnp.int32, (1, HEAD_PAIR), 1) < HEAD_B
    r2 = lax.broadcasted_iota(jnp.int32, (2 * tq, tq), 0)
    causal = lax.broadcasted_iota(jnp.int32, (2 * tq, tq), 1) < jnp.where(r2 >= tq, r2 - tq, r2)
    tri_n = _tri(tq)

    q = q_ref[0]
    zero = jnp.zeros((), q.dtype)
    qh = []
    for p in range(n_pairs):
        qp = q[:, p * HEAD_PAIR:(p + 1) * HEAD_PAIR]
        qh.append(jnp.concatenate([jnp.where(low, qp, zero), jnp.where(low, zero, qp)], axis=0))

    def step(k_blk, v_blk, tri, cs, accs, mask, transposed=False):
        return _sb_step(qh, k_blk, v_blk, tri, cs, accs, mask, transposed)

    def alive(cs):
        m = functools.reduce(jnp.maximum, cs)
        return (jnp.max(m) > SB_DEAD).astype(jnp.int32)

    def sweep(limit, fetch, tri, flag, cs, accs, transposed=False):
        def cond(st):
            return jnp.logical_and(st[0] < limit, st[1] > 0)

        def body(st):
            n, _, cs, accs = st
            k_blk, v_blk = fetch(n)
            cs, accs = step(k_blk, v_blk, tri, cs, accs, None, transposed)
            return n + 1, alive(cs), cs, accs

        _, flag, cs, accs = lax.while_loop(cond, body, (jnp.int32(0), flag, cs, accs))
        return flag, cs, accs

    def fetch_new(n):
        rows = pl.ds(pl.multiple_of((i - 1 - n) * tq, tq), tq)
        return kn_ref[0, rows, :], vn_ref[0, rows, :]

    def fetch_past(n):
        cols = pl.ds(pl.multiple_of((n_past - 1 - n) * tkp, tkp), tkp)
        return kp_ref[:, cols], vp_ref[:, cols]

    own = pl.ds(pl.multiple_of(i * tq, tq), tq)
    cs = tuple(jnp.zeros((2 * tq, 1), F32) for _ in range(n_pairs))
    accs = tuple(jnp.zeros((tq, HEAD_PAIR), F32) for _ in range(n_pairs))
    cs, accs = step(kn_ref[0, own, :], vn_ref[0, own, :], tri_n, cs, accs, causal)
    flag, cs, accs = sweep(i, fetch_new, tri_n, alive(cs), cs, accs)
    if n_past:
        flag, cs, accs = sweep(n_past, fetch_past, _tri(tkp), flag, cs, accs, transposed=True)
    for p in range(n_pairs):
        o_ref[0, :, p * HEAD_PAIR:(p + 1) * HEAD_PAIR] = accs[p].astype(BF16)


def _stick_breaking(q, k_new, v_new, past=None):
    bsz, length, w = q.shape
    tq = _row_tile(length, 128)
    tkp = 128
    qspec = pl.BlockSpec((1, tq, w), lambda b, i: (b, i, 0))
    new_spec = pl.BlockSpec((1, length, w), lambda b, i: (b, 0, 0))
    in_specs = [qspec, new_spec, new_spec]
    args = [q, k_new, v_new]
    n_past = 0
    if past is not None:
        k_past, v_past, j = past
        plen = k_past.shape[3]
        n_past = plen // tkp
        past_spec = pl.BlockSpec((None, None, w, plen), lambda b, i: (j, b, 0, 0))
        in_specs += [past_spec, past_spec]
        args += [k_past, v_past]
    return pl.pallas_call(
        functools.partial(_sb_kernel, tq=tq, tkp=tkp, n_past=n_past, n_pairs=w // HEAD_PAIR),
        grid=(bsz, length // tq),
        in_specs=in_specs,
        out_specs=qspec,
        out_shape=jax.ShapeDtypeStruct((bsz, length, w), BF16),
        compiler_params=_cparams("parallel", "arbitrary"),
        name="stick_breaking",
    )(*args)


def _ret_in_kernel(x_ref, g_ref, w_ref, cos_ref, sin_ref, qkv_ref, gate_ref, *, dk, hc, pre):
    hn = _rms(x_ref[...], g_ref[pre:pre + 1, :]).astype(BF16)
    wq = qkv_ref.shape[1]
    half = dk // 2
    cos = cos_ref[...]
    sin = sin_ref[...]
    cos_k = cos * dk ** -0.5
    sin_k = sin * dk ** -0.5
    for blk in range(w_ref.shape[1] // dk):
        lo = blk * dk
        y = _dot(hn, w_ref[:, lo:lo + dk])
        if blk < 2 * hc:
            c, s = (cos, sin) if blk < hc else (cos_k, sin_k)
            x1 = y[:, :half]
            x2 = y[:, half:]
            qkv_ref[:, lo:lo + half] = (x1 * c - x2 * s).astype(BF16)
            qkv_ref[:, lo + half:lo + dk] = (x2 * c + x1 * s).astype(BF16)
        elif lo < wq:
            qkv_ref[:, lo:lo + dk] = y.astype(BF16)
        else:
            gate_ref[:, lo - wq:lo - wq + dk] = y.astype(BF16)


def _ret_in(x, norm_g, w_in, cos, sin, hc, dk, wv, layer, j, pre):
    n, d = x.shape
    cols = w_in.shape[-1]
    tm = _row_tile(min(n, cos.shape[0]), 512)
    nmod = cos.shape[0] // tm
    tab = pl.BlockSpec((tm, dk // 2), lambda i: (i % nmod, 0))
    return pl.pallas_call(
        functools.partial(_ret_in_kernel, dk=dk, hc=hc, pre=pre),
        grid=(n // tm,),
        in_specs=[
            pl.BlockSpec((tm, d), lambda i: (i, 0)),
            _layer_spec(norm_g, layer),
            _layer_spec(w_in, j, pl.Buffered(1)),
            tab, tab,
        ],
        out_specs=[
            pl.BlockSpec((tm, cols - wv), lambda i: (i, 0)),
            pl.BlockSpec((tm, wv), lambda i: (i, 0)),
        ],
        out_shape=[jax.ShapeDtypeStruct((n, cols - wv), BF16), jax.ShapeDtypeStruct((n, wv), BF16)],
        compiler_params=_cparams("parallel"),
        name="ret_in",
    )(x, norm_g, w_in, cos, sin)


def _ret_kernel(*refs, t, hc, dk, dv, has_state, chained):
    lg_ref, q_ref, k_ref, v_ref = refs[:4]
    r0_ref = refs[4] if has_state else None
    o_ref, rl_ref, r_ref = refs[4 + has_state + chained:]
    l = pl.program_id(1)

    @pl.when(l == 0)
    def _():
        r_ref[...] = r0_ref[...] if has_state else jnp.zeros_like(r_ref)

    ri = lax.broadcasted_iota(jnp.int32, (t, t), 0)
    ci = lax.broadcasted_iota(jnp.int32, (t, t), 1)
    dist = jnp.abs(ri - ci).astype(F32)
    visible = (ci >> CHUNK_BITS) <= (ri >> CHUNK_BITS)
    pos = lax.broadcasted_iota(jnp.int32, (t, 1), 0).astype(F32)
    heads = range(hc)
    lg = [lg_ref[h] for h in heads]
    q = [q_ref[0, :, h * dk:(h + 1) * dk] for h in heads]
    k = [k_ref[0, :, h * dk:(h + 1) * dk] for h in heads]
    v = [v_ref[0, :, h * dv:(h + 1) * dv] for h in heads]
    r = [r_ref[h] for h in heads]
    s = [_dot_nt(q[h], k[h]) for h in heads]
    cross = [_dot(q[h], r[h].astype(BF16)) for h in heads]
    kz = [(k[h].astype(F32) * jnp.exp(lg[h] * (t - 1.0 - pos))).astype(BF16) for h in heads]
    upd = [_dot_tn(kz[h], v[h]) for h in heads]
    for h in heads:
        sm = jnp.where(visible, s[h] * jnp.exp(lg[h] * dist), 0.0).astype(BF16)
        o = _dot(sm, v[h]) + cross[h] * jnp.exp(lg[h] * (pos + 1.0))
        o_ref[0, :, h * dv:(h + 1) * dv] = o.astype(BF16)
        r_ref[h] = jnp.exp(jnp.full((1, 1), t, F32) * lg[h]) * r[h] + upd[h]

    @pl.when(l == pl.num_programs(1) - 1)
    def _():
        rl_ref[...] = r_ref[...]


def _retention(qkv, state, log_g, bsz, length, hc, dk, dv, j, n_layers, finals_prev=None):
    t = _row_tile(length, 256)
    wq, wv = hc * dk, hc * dv
    in_specs = [
        pl.BlockSpec(memory_space=pltpu.SMEM),
        pl.BlockSpec((1, t, wq), lambda b, l: (b, l, 0)),
        pl.BlockSpec((1, t, wq), lambda b, l: (b, l, 1)),
        pl.BlockSpec((1, t, wv), lambda b, l: (b, l, 2 * wq // wv)),
    ]
    args = [log_g, qkv, qkv, qkv]
    state_spec = pl.BlockSpec((None, None, hc, dk, dv), lambda b, l: (j, b, 0, 0, 0))
    if state is not None:
        in_specs.append(state_spec)
        args.append(state)
    aliases = {}
    if finals_prev is not None:
        aliases = {len(args): 1}
        in_specs.append(pl.BlockSpec(memory_space=pl.ANY))
        args.append(finals_prev)
    return pl.pallas_call(
        functools.partial(_ret_kernel, t=t, hc=hc, dk=dk, dv=dv, has_state=state is not None,
                          chained=finals_prev is not None),
        grid=(bsz, length // t),
        in_specs=in_specs,
        out_specs=[pl.BlockSpec((1, t, wv), lambda b, l: (b, l, 0)), state_spec],
        out_shape=[
            jax.ShapeDtypeStruct((bsz, length, wv), BF16),
            jax.ShapeDtypeStruct((n_layers, bsz, hc, dk, dv), F32),
        ],
        input_output_aliases=aliases,
        scratch_shapes=[pltpu.VMEM((hc, dk, dv), F32)],
        compiler_params=_cparams("parallel", "arbitrary"),
        name="retention",
    )(*args)


def _ret_out_kernel(x_ref, o_ref_in, gate_ref, g_ref, w_ref, out_ref, *, hc, dv, post):
    y = None
    for h in range(hc):
        sl = slice(h * dv, (h + 1) * dv)
        o = o_ref_in[:, sl].astype(F32)
        mu = jnp.mean(o, axis=-1, keepdims=True)
        oc = o - mu
        on = oc * lax.rsqrt(jnp.mean(oc * oc, axis=-1, keepdims=True) + EPS)
        gt = gate_ref[:, sl].astype(F32)
        d = _dot((gt * _sigmoid(gt) * on).astype(BF16), w_ref[sl, :])
        y = d if y is None else y + d
    out_ref[...] = x_ref[...] + _rms(y, g_ref[post:post + 1, :])


def _ret_out(x, o, gate, norm_g, w_out, hc, dv, layer, j, post):
    n, d = x.shape
    wv = hc * dv
    tm = _row_tile(n, 512)
    return pl.pallas_call(
        functools.partial(_ret_out_kernel, hc=hc, dv=dv, post=post),
        grid=(n // tm,),
        in_specs=[
            pl.BlockSpec((tm, d), lambda i: (i, 0)),
            pl.BlockSpec((tm, wv), lambda i: (i, 0)),
            pl.BlockSpec((tm, wv), lambda i: (i, 0)),
            _layer_spec(norm_g, layer),
            _layer_spec(w_out, j),
        ],
        out_specs=pl.BlockSpec((tm, d), lambda i: (i, 0)),
        out_shape=jax.ShapeDtypeStruct((n, d), F32),
        compiler_params=_cparams("parallel"),
        name="ret_out",
    )(x, o, gate, norm_g, w_out)


def _block_diag(w):
    n, h, bi, bj = w.shape
    eye = jnp.eye(h, dtype=w.dtype)
    return (eye[None, :, None, :, None] * w[:, :, :, None, :]).reshape(n, h * bi, h * bj)


class _Group:
    def __init__(self, x, p, conv_state, lru_state, past_k, past_v, ret_state, pos0, wts):
        self.bsz, self.length, d = x.shape
        bsz, length = self.bsz, self.length
        self.n = n = bsz * length
        depth = p.shape[0]
        self.x = x.reshape(n, d)
        self.p = p.reshape(depth, n, -1)
        dk = wts["dk"]
        self.w = w = wts["lru"]["lam"].shape[-1]
        pos = (pos0 + jnp.arange(length)).astype(F32)
        half = dk // 2
        freq = ROPE_BASE ** (-jnp.arange(half, dtype=F32) / half)
        ang = pos[:, None] * freq[None, :]
        reps = max(1, min(n, 1024) // length)
        self.cos = jnp.tile(jnp.cos(ang), (reps, 1))
        self.sin = jnp.tile(jnp.sin(ang), (reps, 1))
        self.fresh = conv_state is None
        if self.fresh:
            conv_state = jnp.zeros((1, bsz, CONV_W - 1, w), F32)
            lru_state = jnp.zeros((1, bsz, w), F32)
        self.conv_state = conv_state
        self.lru_state = lru_state.reshape(-1, bsz, 1, w)
        self.past = None
        if not self.fresh:
            plen = past_k.shape[2]
            self.past = (past_k.transpose(0, 1, 3, 4, 2).reshape(-1, bsz, w, plen),
                         past_v.transpose(0, 1, 3, 4, 2).reshape(-1, bsz, w, plen))
        self.ret_state = ret_state
        self.n_even = (depth + 1) // 2
        self.n_odd = depth // 2
        self.new_h, self.new_conv, self.new_ret, self.kv_heads = [], [], None, None

    def mixer(self, i, wts):
        bsz, length, n, w = self.bsz, self.length, self.n, self.w
        norm_g = wts["norm_g"]
        hc, dk, dv = wts["hc"], wts["dk"], wts["dv"]
        seq = lambda a: a.reshape(bsz, length, -1)
        j = i // 2
        if i % 2 == 0:
            js = 0 if self.fresh else j
            xa, ga, q, k, v, *self.kv_heads = _ab_in(self.x, norm_g, wts["ab_w_in"], i, j, 2, self.n_even,
                                                     self.kv_heads)
            oa, h_last, nbuf = _lru(seq(xa), seq(ga), self.conv_state, self.lru_state, wts["lru"], j, js)
            ob = _stick_breaking(seq(q), seq(k), seq(v), None if self.fresh else self.past + (j,))
            self.new_h.append(h_last.reshape(bsz, w))
            self.new_conv.append(nbuf)
            return oa.reshape(n, w), ob.reshape(n, w), wts["ab_w_out"], j, 3
        qkv, gate = _ret_in(self.x, norm_g, wts["ret_w_in"], self.cos, self.sin, hc, dk, hc * dv, i, j, 2)
        o, self.new_ret = _retention(seq(qkv), self.ret_state, wts["log_g"], bsz, length, hc, dk, dv,
                                     j, self.n_odd, self.new_ret)
        self.x = _ret_out(self.x, o.reshape(n, hc * dv), gate, norm_g, wts["ret_w_out"], hc, dv, i, j, 3)
        return None

    def outputs(self):
        d = self.x.shape[1]
        new_k, new_v = (a.reshape(self.n_even, self.bsz, self.length, self.w // HEAD_B, HEAD_B)
                        for a in self.kv_heads)
        return (self.x.reshape(self.bsz, self.length, d), jnp.stack(self.new_h), jnp.stack(self.new_conv),
                new_k, new_v, self.new_ret)


def kernel(x_prompt, x_sample, p_prompt, p_sample, state_lru_h, state_conv, cache_sb_k, cache_sb_v, state_ret,
           norm_g, ffn_w_gate, ffn_w_up, ffn_w_down, ple_w_in, ple_w_gate,
           ab_w_in, ab_w_out, lru_conv_w, lru_conv_b, lru_w_r, lru_b_r, lru_w_i, lru_b_i, lru_lambda,
           ret_w_in, ret_w_out):
    assert norm_g.shape[1] == N_NORMS
    hc, dk, dv = state_ret.shape[2:]
    row = lambda a: a[:, None, :]
    wts = dict(
        hc=hc, dk=dk, dv=dv,
        norm_g=norm_g,
        ple_w_in=ple_w_in.astype(BF16), ple_w_gate=ple_w_gate.astype(BF16),
        ab_w_in=ab_w_in.astype(BF16), ab_w_out=ab_w_out.astype(BF16),
        lru=dict(
            conv_w=lru_conv_w, conv_b=row(lru_conv_b),
            w_r=_block_diag(lru_w_r).astype(BF16), b_r=row(lru_b_r),
            w_i=_block_diag(lru_w_i).astype(BF16), b_i=row(lru_b_i),
            lam=row(lru_lambda),
        ),
        ret_w_in=ret_w_in.astype(BF16), ret_w_out=ret_w_out.astype(BF16),
        log_g=jnp.log(1.0 - 2.0 ** (-5.0 - jnp.arange(hc, dtype=F32))),
    )
    prompt = _Group(x_prompt, p_prompt, None, None, None, None, None, 0, wts)
    sample = _Group(x_sample, p_sample, state_conv, state_lru_h, cache_sb_k, cache_sb_v, state_ret,
                    cache_sb_k.shape[2], wts)
    for i in range(norm_g.shape[0]):
        for which, (pre, post) in enumerate(((0, 1), (4, 5))):
            tails = {}
            for g in (sample, prompt):
                if which == 1:
                    tails[g] = dict(mix=g.mixer(i, wts), ple=(g.p, wts["ple_w_gate"], wts["ple_w_in"], 6, 7))
                else:
                    tails[g] = {}
            sample.x, wg, wu, wd = _ffn(sample.x, norm_g, ffn_w_gate, ffn_w_up, ffn_w_down, i, (i, which),
                                        pre, post, emit=True, **tails[sample])
            prompt.x = _ffn(prompt.x, norm_g, wg, wu, wd, i, (), pre, post, **tails[prompt])
    y_p, h_p, c_p, k_p, v_p, r_p = prompt.outputs()
    y_s, h_s, c_s, k_s, v_s, r_s = sample.outputs()
    return (y_p, y_s, h_p, c_p, k_p, v_p, r_p, h_s, c_s, k_s, v_s, r_s)
```

```python
import functools

import jax
import jax.numpy as jnp
from jax import lax
from jax.experimental import pallas as pl
from jax.experimental.pallas import tpu as pltpu

F32 = jnp.float32
BF16 = jnp.bfloat16

EPS = 1e-6
LRU_C = 8.0
CONV_W = 4
CHUNK_BITS = 6
ROPE_BASE = 10000.0
HEAD_B = 64
HEAD_PAIR = 2 * HEAD_B
SB_DEAD = -40.0
SUBLANES = 8
FF_SUB = 256
VMEM_LIMIT_BYTES = 56 * 1024 * 1024
N_NORMS = 8


def _cparams(*sem):
    return pltpu.CompilerParams(dimension_semantics=sem, vmem_limit_bytes=VMEM_LIMIT_BYTES)


def _dot(a, b):
    return jnp.dot(a, b, preferred_element_type=F32)


def _dot_nt(a, b):
    return lax.dot_general(a, b, (((1,), (1,)), ((), ())), preferred_element_type=F32)


def _dot_tn(a, b):
    return lax.dot_general(a, b, (((0,), (0,)), ((), ())), preferred_element_type=F32)


def _rms(x, g):
    return x * lax.rsqrt(jnp.mean(x * x, axis=-1, keepdims=True) + EPS) * g


def _sigmoid(x):
    return 1.0 / (1.0 + jnp.exp(-x))


def _softplus(x):
    return jnp.maximum(x, 0.0) + jnp.log(1.0 + jnp.exp(-jnp.abs(x)))


def _gelu_tanh(x):
    return 0.5 * x * (1.0 + jnp.tanh(0.7978845608028654 * (x + 0.044715 * x * x * x)))


def _row_tile(n, pref):
    t = min(n, pref)
    while n % t:
        t //= 2
    return t


def _layer_spec(arr, layer, pipeline_mode=None):
    layer = layer if isinstance(layer, tuple) else (layer,)
    rest = arr.shape[len(layer):]
    idx = layer + (0,) * len(rest)
    return pl.BlockSpec((None,) * len(layer) + rest, lambda *_: idx, pipeline_mode=pipeline_mode)


def _ffn_kernel(*refs, n_sub, n_steps, emit, pre, post, mix_post, ple_norms):
    refs = list(refs)
    x_ref, g_ref, wg_ref, wu_ref, wd_ref = refs[:5]
    n_out = 4 if emit else 1
    n_scratch = 1 if n_steps == 1 else 2
    extra = refs[5:len(refs) - n_out - n_scratch]
    o_ref = refs[len(refs) - n_out - n_scratch]
    bf_refs = refs[len(refs) - n_scratch - 3:len(refs) - n_scratch] if emit else (None, None, None)
    acc_ref = refs[len(refs) - n_scratch]
    xn_ref = refs[-1] if n_steps > 1 else None
    step = pl.program_id(1)

    def head():
        x = x_ref[...]
        if mix_post is not None:
            a_ref, b_ref, wm_ref = extra[:3]
            wa = a_ref.shape[1]
            y = _dot(a_ref[...], wm_ref[0:wa, :]) + _dot(b_ref[...], wm_ref[wa:, :])
            x = x + _rms(y, g_ref[mix_post:mix_post + 1, :])
        o_ref[...] = x
        xn = _rms(x, g_ref[pre:pre + 1, :]).astype(BF16)
        if xn_ref is not None:
            xn_ref[...] = xn
            acc_ref[...] = jnp.zeros_like(acc_ref)
        return xn

    def tail():
        x = o_ref[...] + 0.5 * _rms(acc_ref[...], g_ref[post:post + 1, :])
        if ple_norms is not None:
            p_ref, wgate_ref, win_ref = extra[-3:]
            pre2, post2 = ple_norms
            gate = _sigmoid(_dot(_rms(x, g_ref[pre2:pre2 + 1, :]).astype(BF16), wgate_ref[...]))
            e = _dot(p_ref[...].astype(BF16), win_ref[...])
            x = x + _rms(gate * e, g_ref[post2:post2 + 1, :])
        o_ref[...] = x

    def weights(w_ref, bf_ref, idx):
        w = w_ref[idx]
        if emit:
            w = w.astype(BF16)
            bf_ref[idx] = w
        return w

    if n_steps == 1:
        xn = head()
    else:
        @pl.when(step == 0)
        def _():
            head()

        xn = xn_ref[...]
    for c in range(n_sub):
        sl = pl.ds(c * FF_SUB, FF_SUB)
        h = _dot(xn, weights(wg_ref, bf_refs[0], (slice(None), sl)))
        u = _dot(xn, weights(wu_ref, bf_refs[1], (slice(None), sl)))
        a = (h * _sigmoid(h) * u).astype(BF16)
        d = _dot(a, weights(wd_ref, bf_refs[2], (sl, slice(None))))
        if c == 0 and n_steps == 1:
            acc_ref[...] = d
        else:
            acc_ref[...] += d
    if n_steps == 1:
        tail()
    else:
        @pl.when(step == n_steps - 1)
        def _():
            tail()


def _ffn(x, norm_g, wg, wu, wd, layer, which, pre, post, mix=None, ple=None, emit=False):
    n, d = x.shape
    f = wg.shape[-1]
    tm = _row_tile(n, 1024)
    tf = FF_SUB if emit else f
    n_steps = f // tf
    lead = (None,) * len(which)
    once = None if emit else pl.Buffered(1)
    rows = lambda width: pl.BlockSpec((tm, width), lambda i, s: (i, 0))
    in_specs = [
        rows(d),
        _layer_spec(norm_g, layer),
        pl.BlockSpec(lead + (d, tf), lambda i, s: which + (0, s), pipeline_mode=once),
        pl.BlockSpec(lead + (d, tf), lambda i, s: which + (0, s), pipeline_mode=once),
        pl.BlockSpec(lead + (tf, d), lambda i, s: which + (s, 0), pipeline_mode=once),
    ]
    args = [x, norm_g, wg, wu, wd]
    mix_post = ple_norms = None
    resident = pl.Buffered(1)
    if mix is not None:
        a, b, wm, j, mix_post = mix
        in_specs += [rows(a.shape[1]), rows(b.shape[1]), _layer_spec(wm, j, resident)]
        args += [a, b, wm]
    if ple is not None:
        p, wgate, win, pre2, post2 = ple
        ple_norms = (pre2, post2)
        in_specs += [
            pl.BlockSpec((None, tm, p.shape[-1]), lambda i, s: (layer, i, 0)),
            _layer_spec(wgate, layer, resident),
            _layer_spec(win, layer, resident),
        ]
        args += [p, wgate, win]
    out_specs = [rows(d)]
    out_shape = [jax.ShapeDtypeStruct((n, d), F32)]
    scratch = [pltpu.VMEM((tm, d), F32)]
    if emit:
        assert n == tm, "the weight copies are written once, by a single row tile"
        out_specs += [
            pl.BlockSpec((d, tf), lambda i, s: (0, s)),
            pl.BlockSpec((d, tf), lambda i, s: (0, s)),
            pl.BlockSpec((tf, d), lambda i, s: (s, 0)),
        ]
        out_shape += [jax.ShapeDtypeStruct((d, f), BF16)] * 2 + [jax.ShapeDtypeStruct((f, d), BF16)]
    if n_steps > 1:
        scratch.append(pltpu.VMEM((tm, d), BF16))
    out = pl.pallas_call(
        functools.partial(_ffn_kernel, n_sub=tf // FF_SUB, n_steps=n_steps, emit=emit, pre=pre, post=post,
                          mix_post=mix_post, ple_norms=ple_norms),
        grid=(n // tm, n_steps),
        in_specs=in_specs,
        out_specs=out_specs,
        out_shape=out_shape,
        scratch_shapes=scratch,
        compiler_params=_cparams("parallel", "arbitrary"),
        name="ffn" + ("_mix" if mix is not None else "") + ("_ple" if ple is not None else "")
        + ("_cast" if emit else ""),
    )(*args)
    return out if emit else out[0]


def _ab_in_kernel(*refs, w, pre, n_heads, chained):
    x_ref, g_ref, w_ref = refs[:3]
    xa_ref, ga_ref, q_ref, k_ref, v_ref, kh_ref, vh_ref = refs[5 if chained else 3:]
    hn = _rms(x_ref[...], g_ref[pre:pre + 1, :]).astype(BF16)
    for idx, o_ref in enumerate((xa_ref, ga_ref)):
        o_ref[...] = _dot(hn, w_ref[:, idx * w:(idx + 1) * w])
    hd = w // n_heads
    q_ref[...] = (_dot(hn, w_ref[:, 2 * w:3 * w]) * hd ** -0.5).astype(BF16)
    tm = x_ref.shape[0]
    for idx, o_ref, oh_ref in ((3, k_ref, kh_ref), (4, v_ref, vh_ref)):
        y = _dot(hn, w_ref[:, idx * w:(idx + 1) * w])
        o_ref[...] = y.astype(BF16)
        for h in range(n_heads):
            oh_ref[pl.ds(h, tm, stride=n_heads), :] = y[:, h * hd:(h + 1) * hd]


def _ab_in(x, norm_g, w_in, layer, j, pre, n_layers, heads_prev=None):
    n, d = x.shape
    w = w_in.shape[-1] // 5
    n_heads = w // HEAD_B
    tm = _row_tile(n, 512)
    out = jax.ShapeDtypeStruct((n, w), F32)
    out_b = jax.ShapeDtypeStruct((n, w), BF16)
    out_h = jax.ShapeDtypeStruct((n_layers, n * n_heads, HEAD_B), F32)
    ospec = pl.BlockSpec((tm, w), lambda i: (i, 0))
    hspec = pl.BlockSpec((None, tm * n_heads, HEAD_B), lambda i: (j, i, 0))
    in_specs = [pl.BlockSpec((tm, d), lambda i: (i, 0)), _layer_spec(norm_g, layer), _layer_spec(w_in, j)]
    args = [x, norm_g, w_in]
    aliases = {}
    if heads_prev is not None:
        in_specs += [pl.BlockSpec(memory_space=pl.ANY)] * 2
        args += list(heads_prev)
        aliases = {3: 5, 4: 6}
    return pl.pallas_call(
        functools.partial(_ab_in_kernel, w=w, pre=pre, n_heads=n_heads, chained=heads_prev is not None),
        grid=(n // tm,),
        in_specs=in_specs,
        out_specs=[ospec] * 5 + [hspec] * 2,
        out_shape=[out] * 2 + [out_b] * 3 + [out_h] * 2,
        input_output_aliases=aliases,
        compiler_params=_cparams("parallel"),
        name="ab_in",
    )(*args)


def _lru_kernel(xa_ref, ga_ref, buf_ref, h0_ref, cw_ref, cb_ref, wr_ref, br_ref, wi_ref, bi_ref, lam_ref,
                oa_ref, hl_ref, nb_ref, hc_ref, xc_ref, *, t):
    l = pl.program_id(1)

    @pl.when(l == 0)
    def _():
        hc_ref[...] = h0_ref[...]
        xc_ref[...] = jnp.zeros_like(xc_ref)
        xc_ref[SUBLANES - (CONV_W - 1):, :] = buf_ref[...]

    x = xa_ref[0]
    prev = xc_ref[...]
    row8 = lax.broadcasted_iota(jnp.int32, (SUBLANES, 1), 0)
    cw = cw_ref[...]
    y = cb_ref[...] + x * cw[CONV_W - 1:CONV_W]
    for k in range(1, CONV_W):
        xs = pltpu.roll(x, k, 0)
        head = jnp.where(row8 < k, pltpu.roll(prev, k, 0), xs[:SUBLANES])
        xs = jnp.concatenate([head, xs[SUBLANES:]], axis=0)
        y = y + xs * cw[CONV_W - 1 - k:CONV_W - k]
    xc_ref[...] = x[t - SUBLANES:]

    yb = y.astype(BF16)
    rg = _sigmoid(_dot(yb, wr_ref[...]) + br_ref[...])
    ig = _sigmoid(_dot(yb, wi_ref[...]) + bi_ref[...])
    log_a = (-LRU_C) * rg * _softplus(-lam_ref[...])
    a = jnp.exp(log_a)
    b = jnp.sqrt(1.0 - jnp.exp(2.0 * log_a)) * ig * y

    row = lax.broadcasted_iota(jnp.int32, (t, 1), 0)
    d = 1
    while d < t:
        keep = row >= d
        a_sh = jnp.where(keep, pltpu.roll(a, d, 0), 1.0)
        b_sh = jnp.where(keep, pltpu.roll(b, d, 0), 0.0)
        b = a * b_sh + b
        a = a * a_sh
        d *= 2
    h = a * hc_ref[...] + b
    hc_ref[...] = h[t - 1:t]
    oa_ref[0] = (_gelu_tanh(ga_ref[0]) * h).astype(BF16)

    @pl.when(l == pl.num_programs(1) - 1)
    def _():
        hl_ref[0] = h[t - 1:t]
        nb_ref[0] = xa_ref[0, t - (CONV_W - 1):t, :]


def _lru(xa, ga, buf, h0, lp, j, js):
    bsz, length, w = xa.shape
    t = _row_tile(length, 256)
    seq = pl.BlockSpec((1, t, w), lambda b, l: (b, l, 0))
    par = lambda a: _layer_spec(a, j)
    return pl.pallas_call(
        functools.partial(_lru_kernel, t=t),
        grid=(bsz, length // t),
        in_specs=[
            seq, seq,
            pl.BlockSpec((None, None, CONV_W - 1, w), lambda b, l: (js, b, 0, 0)),
            pl.BlockSpec((None, None, 1, w), lambda b, l: (js, b, 0, 0)),
            par(lp["conv_w"]), par(lp["conv_b"]), par(lp["w_r"]), par(lp["b_r"]),
            par(lp["w_i"]), par(lp["b_i"]), par(lp["lam"]),
        ],
        out_specs=[
            seq,
            pl.BlockSpec((1, 1, w), lambda b, l: (b, 0, 0)),
            pl.BlockSpec((1, CONV_W - 1, w), lambda b, l: (b, 0, 0)),
        ],
        out_shape=[
            jax.ShapeDtypeStruct((bsz, length, w), BF16),
            jax.ShapeDtypeStruct((bsz, 1, w), F32),
            jax.ShapeDtypeStruct((bsz, CONV_W - 1, w), F32),
        ],
        scratch_shapes=[pltpu.VMEM((1, w), F32), pltpu.VMEM((SUBLANES, w), F32)],
        compiler_params=_cparams("parallel", "arbitrary"),
        name="lru",
    )(xa, ga, buf, h0, lp["conv_w"], lp["conv_b"], lp["w_r"], lp["b_r"], lp["w_i"], lp["b_i"], lp["lam"])


def _sb_step(qh, blocks, tri, cs, accs, transposed=False):
    n_pairs = len(qh)
    tq = accs[0].shape[0]
    pair = lambda p: slice(p * HEAD_PAIR, (p + 1) * HEAD_PAIR)
    if transposed:
        zs = [[_dot(qh[p], k[pair(p), :].astype(BF16)) for p in range(n_pairs)] for k, _, _ in blocks]
        low = lax.broadcasted_iota(jnp.int32, (HEAD_PAIR, 1), 0) < HEAD_B
    else:
        zs = [[_dot_nt(qh[p], k[:, pair(p)].astype(BF16)) for p in range(n_pairs)] for k, _, _ in blocks]
        low = lax.broadcasted_iota(jnp.int32, (1, HEAD_PAIR), 1) < HEAD_B
    ls = [[-_softplus(z) if m is None else jnp.where(m, -_softplus(z), 0.0) for z in zb]
          for zb, (_, _, m) in zip(zs, blocks)]
    flat = [l for lb in ls for l in lb]
    his = [l.astype(BF16) for l in flat]
    los = [(l - hi.astype(F32)).astype(BF16) for l, hi in zip(flat, his)]
    sums = _dot(jnp.concatenate(his + los, axis=0), tri)
    rows = 2 * tq
    half = len(flat) * rows
    zero = jnp.zeros((), BF16)
    cs = list(cs)
    es = [[] for _ in range(n_pairs)]
    vs = [[] for _ in range(n_pairs)]
    for b, (_, v_blk, m) in enumerate(blocks):
        for p in range(n_pairs):
            at = (b * n_pairs + p) * rows
            incl = sums[at:at + rows] + sums[half + at:half + at + rows]
            e = jnp.exp(zs[b][p] + cs[p] + incl)
            if m is not None:
                e = jnp.where(m, e, 0.0)
            e = e.astype(BF16)
            es[p] += [e[:tq], e[tq:]]
            cs[p] = cs[p] + incl[:, 0:1]
            vp = (v_blk[pair(p), :] if transposed else v_blk[:, pair(p)]).astype(BF16)
            vs[p] += [jnp.where(low, vp, zero), jnp.where(low, zero, vp)]
    accs_out = []
    for p in range(n_pairs):
        e2 = jnp.concatenate(es[p], axis=1)
        if transposed:
            accs_out.append(accs[p] + _dot_nt(e2, jnp.concatenate(vs[p], axis=1)))
        else:
            accs_out.append(accs[p] + _dot(e2, jnp.concatenate(vs[p], axis=0)))
    return tuple(cs), tuple(accs_out)


def _tri(n):
    r = lax.broadcasted_iota(jnp.int32, (n, n), 0)
    c = lax.broadcasted_iota(jnp.int32, (n, n), 1)
    return jnp.where(r >= c, 1.0, 0.0).astype(BF16)


def _sb_kernel(*refs, tq, tkp, n_past, n_pairs, with_prev):
    if n_past:
        q_ref, kn_ref, vn_ref, kp_ref, vp_ref, o_ref = refs
    else:
        q_ref, kn_ref, vn_ref, o_ref = refs
    i = pl.program_id(1)
    low = lax.broadcasted_iota(jnp.int32, (1, HEAD_PAIR), 1) < HEAD_B
    r2 = lax.broadcasted_iota(jnp.int32, (2 * tq, tq), 0)
    causal = lax.broadcasted_iota(jnp.int32, (2 * tq, tq), 1) < jnp.where(r2 >= tq, r2 - tq, r2)
    tri_n = _tri(tq)

    q = q_ref[0]
    zero = jnp.zeros((), q.dtype)
    qh = []
    for p in range(n_pairs):
        qp = q[:, p * HEAD_PAIR:(p + 1) * HEAD_PAIR]
        qh.append(jnp.concatenate([jnp.where(low, qp, zero), jnp.where(low, zero, qp)], axis=0))

    def alive(cs):
        m = functools.reduce(jnp.maximum, cs)
        return (jnp.max(m) > SB_DEAD).astype(jnp.int32)

    def sweep(first, limit, fetch, tri, flag, cs, accs, transposed=False):
        def cond(st):
            return jnp.logical_and(st[0] < limit, st[1] > 0)

        def body(st):
            n, _, cs, accs = st
            cs, accs = _sb_step(qh, [fetch(n) + (None,)], tri, cs, accs, transposed)
            return n + 1, alive(cs), cs, accs

        _, flag, cs, accs = lax.while_loop(cond, body, (jnp.int32(first), flag, cs, accs))
        return flag, cs, accs

    def fetch_new(n):
        rows = pl.ds(pl.multiple_of(jnp.maximum(i - 1 - n, 0) * tq, tq), tq)
        return kn_ref[0, rows, :], vn_ref[0, rows, :]

    def fetch_past(n):
        cols = pl.ds(pl.multiple_of((n_past - 1 - n) * tkp, tkp), tkp)
        return kp_ref[:, cols], vp_ref[:, cols]

    own = pl.ds(pl.multiple_of(i * tq, tq), tq)
    cs = tuple(jnp.zeros((2 * tq, 1), F32) for _ in range(n_pairs))
    accs = tuple(jnp.zeros((tq, HEAD_PAIR), F32) for _ in range(n_pairs))
    blocks = [(kn_ref[0, own, :], vn_ref[0, own, :], causal)]
    if with_prev:
        blocks.append(fetch_new(0) + (jnp.broadcast_to(i > 0, causal.shape),))
    cs, accs = _sb_step(qh, blocks, tri_n, cs, accs)
    flag, cs, accs = sweep(len(blocks) - 1, i, fetch_new, tri_n, alive(cs), cs, accs)
    if n_past:
        flag, cs, accs = sweep(0, n_past, fetch_past, _tri(tkp), flag, cs, accs, transposed=True)
    for p in range(n_pairs):
        o_ref[0, :, p * HEAD_PAIR:(p + 1) * HEAD_PAIR] = accs[p].astype(BF16)


def _stick_breaking(q, k_new, v_new, past=None):
    bsz, length, w = q.shape
    tq = _row_tile(length, 128)
    tkp = 128
    qspec = pl.BlockSpec((1, tq, w), lambda b, i: (b, i, 0))
    new_spec = pl.BlockSpec((1, length, w), lambda b, i: (b, 0, 0))
    in_specs = [qspec, new_spec, new_spec]
    args = [q, k_new, v_new]
    n_past = 0
    if past is not None:
        k_past, v_past, j = past
        plen = k_past.shape[3]
        n_past = plen // tkp
        past_spec = pl.BlockSpec((None, None, w, plen), lambda b, i: (j, b, 0, 0))
        in_specs += [past_spec, past_spec]
        args += [k_past, v_past]
    return pl.pallas_call(
        functools.partial(_sb_kernel, tq=tq, tkp=tkp, n_past=n_past, n_pairs=w // HEAD_PAIR,
                          with_prev=length > tq),
        grid=(bsz, length // tq),
        in_specs=in_specs,
        out_specs=qspec,
        out_shape=jax.ShapeDtypeStruct((bsz, length, w), BF16),
        compiler_params=_cparams("parallel", "arbitrary"),
        name="stick_breaking",
    )(*args)


def _ret_in_kernel(x_ref, g_ref, w_ref, cos_ref, sin_ref, qkv_ref, gate_ref, *, dk, hc, pre):
    hn = _rms(x_ref[...], g_ref[pre:pre + 1, :]).astype(BF16)
    wq = qkv_ref.shape[1]
    half = dk // 2
    cos = cos_ref[...]
    sin = sin_ref[...]
    cos_k = cos * dk ** -0.5
    sin_k = sin * dk ** -0.5
    for blk in range(w_ref.shape[1] // dk):
        lo = blk * dk
        y = _dot(hn, w_ref[:, lo:lo + dk])
        if blk < 2 * hc:
            c, s = (cos, sin) if blk < hc else (cos_k, sin_k)
            x1 = y[:, :half]
            x2 = y[:, half:]
            qkv_ref[:, lo:lo + half] = (x1 * c - x2 * s).astype(BF16)
            qkv_ref[:, lo + half:lo + dk] = (x2 * c + x1 * s).astype(BF16)
        elif lo < wq:
            qkv_ref[:, lo:lo + dk] = y.astype(BF16)
        else:
            gate_ref[:, lo - wq:lo - wq + dk] = y.astype(BF16)


def _ret_in(x, norm_g, w_in, cos, sin, hc, dk, wv, layer, j, pre):
    n, d = x.shape
    cols = w_in.shape[-1]
    tm = _row_tile(min(n, cos.shape[0]), 512)
    nmod = cos.shape[0] // tm
    tab = pl.BlockSpec((tm, dk // 2), lambda i: (i % nmod, 0))
    return pl.pallas_call(
        functools.partial(_ret_in_kernel, dk=dk, hc=hc, pre=pre),
        grid=(n // tm,),
        in_specs=[
            pl.BlockSpec((tm, d), lambda i: (i, 0)),
            _layer_spec(norm_g, layer),
            _layer_spec(w_in, j, pl.Buffered(1)),
            tab, tab,
        ],
        out_specs=[
            pl.BlockSpec((tm, cols - wv), lambda i: (i, 0)),
            pl.BlockSpec((tm, wv), lambda i: (i, 0)),
        ],
        out_shape=[jax.ShapeDtypeStruct((n, cols - wv), BF16), jax.ShapeDtypeStruct((n, wv), BF16)],
        compiler_params=_cparams("parallel"),
        name="ret_in",
    )(x, norm_g, w_in, cos, sin)


def _ret_kernel(*refs, t, hc, dk, dv, has_state, chained):
    lg_ref, q_ref, k_ref, v_ref = refs[:4]
    r0_ref = refs[4] if has_state else None
    o_ref, rl_ref, r_ref = refs[4 + has_state + chained:]
    l = pl.program_id(1)

    @pl.when(l == 0)
    def _():
        r_ref[...] = r0_ref[...] if has_state else jnp.zeros_like(r_ref)

    ri = lax.broadcasted_iota(jnp.int32, (t, t), 0)
    ci = lax.broadcasted_iota(jnp.int32, (t, t), 1)
    dist = jnp.abs(ri - ci).astype(F32)
    visible = (ci >> CHUNK_BITS) <= (ri >> CHUNK_BITS)
    pos = lax.broadcasted_iota(jnp.int32, (t, 1), 0).astype(F32)
    heads = range(hc)
    lg = [lg_ref[h] for h in heads]
    q = [q_ref[0, :, h * dk:(h + 1) * dk] for h in heads]
    k = [k_ref[0, :, h * dk:(h + 1) * dk] for h in heads]
    v = [v_ref[0, :, h * dv:(h + 1) * dv] for h in heads]
    r = [r_ref[h] for h in heads]
    s = [_dot_nt(q[h], k[h]) for h in heads]
    cross = [_dot(q[h], r[h].astype(BF16)) for h in heads]
    kz = [(k[h].astype(F32) * jnp.exp(lg[h] * (t - 1.0 - pos))).astype(BF16) for h in heads]
    upd = [_dot_tn(kz[h], v[h]) for h in heads]
    for h in heads:
        sm = jnp.where(visible, s[h] * jnp.exp(lg[h] * dist), 0.0).astype(BF16)
        o = _dot(sm, v[h]) + cross[h] * jnp.exp(lg[h] * (pos + 1.0))
        o_ref[0, :, h * dv:(h + 1) * dv] = o.astype(BF16)
        r_ref[h] = jnp.exp(jnp.full((1, 1), t, F32) * lg[h]) * r[h] + upd[h]

    @pl.when(l == pl.num_programs(1) - 1)
    def _():
        rl_ref[...] = r_ref[...]


def _retention(qkv, state, log_g, bsz, length, hc, dk, dv, j, n_layers, finals_prev=None):
    t = _row_tile(length, 256)
    wq, wv = hc * dk, hc * dv
    in_specs = [
        pl.BlockSpec(memory_space=pltpu.SMEM),
        pl.BlockSpec((1, t, wq), lambda b, l: (b, l, 0)),
        pl.BlockSpec((1, t, wq), lambda b, l: (b, l, 1)),
        pl.BlockSpec((1, t, wv), lambda b, l: (b, l, 2 * wq // wv)),
    ]
    args = [log_g, qkv, qkv, qkv]
    state_spec = pl.BlockSpec((None, None, hc, dk, dv), lambda b, l: (j, b, 0, 0, 0))
    if state is not None:
        in_specs.append(state_spec)
        args.append(state)
    aliases = {}
    if finals_prev is not None:
        aliases = {len(args): 1}
        in_specs.append(pl.BlockSpec(memory_space=pl.ANY))
        args.append(finals_prev)
    return pl.pallas_call(
        functools.partial(_ret_kernel, t=t, hc=hc, dk=dk, dv=dv, has_state=state is not None,
                          chained=finals_prev is not None),
        grid=(bsz, length // t),
        in_specs=in_specs,
        out_specs=[pl.BlockSpec((1, t, wv), lambda b, l: (b, l, 0)), state_spec],
        out_shape=[
            jax.ShapeDtypeStruct((bsz, length, wv), BF16),
            jax.ShapeDtypeStruct((n_layers, bsz, hc, dk, dv), F32),
        ],
        input_output_aliases=aliases,
        scratch_shapes=[pltpu.VMEM((hc, dk, dv), F32)],
        compiler_params=_cparams("parallel", "arbitrary"),
        name="retention",
    )(*args)


def _ret_out_kernel(x_ref, o_ref_in, gate_ref, g_ref, w_ref, out_ref, *, hc, dv, post):
    y = None
    for h in range(hc):
        sl = slice(h * dv, (h + 1) * dv)
        o = o_ref_in[:, sl].astype(F32)
        mu = jnp.mean(o, axis=-1, keepdims=True)
        oc = o - mu
        on = oc * lax.rsqrt(jnp.mean(oc * oc, axis=-1, keepdims=True) + EPS)
        gt = gate_ref[:, sl].astype(F32)
        d = _dot((gt * _sigmoid(gt) * on).astype(BF16), w_ref[sl, :])
        y = d if y is None else y + d
    out_ref[...] = x_ref[...] + _rms(y, g_ref[post:post + 1, :])


def _ret_out(x, o, gate, norm_g, w_out, hc, dv, layer, j, post):
    n, d = x.shape
    wv = hc * dv
    tm = _row_tile(n, 512)
    return pl.pallas_call(
        functools.partial(_ret_out_kernel, hc=hc, dv=dv, post=post),
        grid=(n // tm,),
        in_specs=[
            pl.BlockSpec((tm, d), lambda i: (i, 0)),
            pl.BlockSpec((tm, wv), lambda i: (i, 0)),
            pl.BlockSpec((tm, wv), lambda i: (i, 0)),
            _layer_spec(norm_g, layer),
            _layer_spec(w_out, j),
        ],
        out_specs=pl.BlockSpec((tm, d), lambda i: (i, 0)),
        out_shape=jax.ShapeDtypeStruct((n, d), F32),
        compiler_params=_cparams("parallel"),
        name="ret_out",
    )(x, o, gate, norm_g, w_out)


def _block_diag(w):
    n, h, bi, bj = w.shape
    eye = jnp.eye(h, dtype=w.dtype)
    return (eye[None, :, None, :, None] * w[:, :, :, None, :]).reshape(n, h * bi, h * bj)


class _Group:
    def __init__(self, x, p, conv_state, lru_state, past_k, past_v, ret_state, pos0, wts):
        self.bsz, self.length, d = x.shape
        bsz, length = self.bsz, self.length
        self.n = n = bsz * length
        depth = p.shape[0]
        self.x = x.reshape(n, d)
        self.p = p.reshape(depth, n, -1)
        dk = wts["dk"]
        self.w = w = wts["lru"]["lam"].shape[-1]
        pos = (pos0 + jnp.arange(length)).astype(F32)
        half = dk // 2
        freq = ROPE_BASE ** (-jnp.arange(half, dtype=F32) / half)
        ang = pos[:, None] * freq[None, :]
        reps = max(1, min(n, 1024) // length)
        self.cos = jnp.tile(jnp.cos(ang), (reps, 1))
        self.sin = jnp.tile(jnp.sin(ang), (reps, 1))
        self.fresh = conv_state is None
        if self.fresh:
            conv_state = jnp.zeros((1, bsz, CONV_W - 1, w), F32)
            lru_state = jnp.zeros((1, bsz, w), F32)
        self.conv_state = conv_state
        self.lru_state = lru_state.reshape(-1, bsz, 1, w)
        self.past = None
        if not self.fresh:
            plen = past_k.shape[2]
            self.past = (past_k.transpose(0, 1, 3, 4, 2).reshape(-1, bsz, w, plen),
                         past_v.transpose(0, 1, 3, 4, 2).reshape(-1, bsz, w, plen))
        self.ret_state = ret_state
        self.n_even = (depth + 1) // 2
        self.n_odd = depth // 2
        self.new_h, self.new_conv, self.new_ret, self.kv_heads = [], [], None, None

    def mixer(self, i, wts):
        bsz, length, n, w = self.bsz, self.length, self.n, self.w
        norm_g = wts["norm_g"]
        hc, dk, dv = wts["hc"], wts["dk"], wts["dv"]
        seq = lambda a: a.reshape(bsz, length, -1)
        j = i // 2
        if i % 2 == 0:
            js = 0 if self.fresh else j
            xa, ga, q, k, v, *self.kv_heads = _ab_in(self.x, norm_g, wts["ab_w_in"], i, j, 2, self.n_even,
                                                     self.kv_heads)
            oa, h_last, nbuf = _lru(seq(xa), seq(ga), self.conv_state, self.lru_state, wts["lru"], j, js)
            ob = _stick_breaking(seq(q), seq(k), seq(v), None if self.fresh else self.past + (j,))
            self.new_h.append(h_last.reshape(bsz, w))
            self.new_conv.append(nbuf)
            return oa.reshape(n, w), ob.reshape(n, w), wts["ab_w_out"], j, 3
        qkv, gate = _ret_in(self.x, norm_g, wts["ret_w_in"], self.cos, self.sin, hc, dk, hc * dv, i, j, 2)
        o, self.new_ret = _retention(seq(qkv), self.ret_state, wts["log_g"], bsz, length, hc, dk, dv,
                                     j, self.n_odd, self.new_ret)
        self.x = _ret_out(self.x, o.reshape(n, hc * dv), gate, norm_g, wts["ret_w_out"], hc, dv, i, j, 3)
        return None

    def outputs(self):
        d = self.x.shape[1]
        new_k, new_v = (a.reshape(self.n_even, self.bsz, self.length, self.w // HEAD_B, HEAD_B)
                        for a in self.kv_heads)
        return (self.x.reshape(self.bsz, self.length, d), jnp.stack(self.new_h), jnp.stack(self.new_conv),
                new_k, new_v, self.new_ret)


def kernel(x_prompt, x_sample, p_prompt, p_sample, state_lru_h, state_conv, cache_sb_k, cache_sb_v, state_ret,
           norm_g, ffn_w_gate, ffn_w_up, ffn_w_down, ple_w_in, ple_w_gate,
           ab_w_in, ab_w_out, lru_conv_w, lru_conv_b, lru_w_r, lru_b_r, lru_w_i, lru_b_i, lru_lambda,
           ret_w_in, ret_w_out):
    assert norm_g.shape[1] == N_NORMS
    hc, dk, dv = state_ret.shape[2:]
    row = lambda a: a[:, None, :]
    wts = dict(
        hc=hc, dk=dk, dv=dv,
        norm_g=norm_g,
        ple_w_in=ple_w_in.astype(BF16), ple_w_gate=ple_w_gate.astype(BF16),
        ab_w_in=ab_w_in.astype(BF16), ab_w_out=ab_w_out.astype(BF16),
        lru=dict(
            conv_w=lru_conv_w, conv_b=row(lru_conv_b),
            w_r=_block_diag(lru_w_r).astype(BF16), b_r=row(lru_b_r),
            w_i=_block_diag(lru_w_i).astype(BF16), b_i=row(lru_b_i),
            lam=row(lru_lambda),
        ),
        ret_w_in=ret_w_in.astype(BF16), ret_w_out=ret_w_out.astype(BF16),
        log_g=jnp.log(1.0 - 2.0 ** (-5.0 - jnp.arange(hc, dtype=F32))),
    )
    prompt = _Group(x_prompt, p_prompt, None, None, None, None, None, 0, wts)
    sample = _Group(x_sample, p_sample, state_conv, state_lru_h, cache_sb_k, cache_sb_v, state_ret,
                    cache_sb_k.shape[2], wts)
    for i in range(norm_g.shape[0]):
        for which, (pre, post) in enumerate(((0, 1), (4, 5))):
            tails = {}
            for g in (sample, prompt):
                if which == 1:
                    tails[g] = dict(mix=g.mixer(i, wts), ple=(g.p, wts["ple_w_gate"], wts["ple_w_in"], 6, 7))
                else:
                    tails[g] = {}
            sample.x, wg, wu, wd = _ffn(sample.x, norm_g, ffn_w_gate, ffn_w_up, ffn_w_down, i, (i, which),
                                        pre, post, emit=True, **tails[sample])
            prompt.x = _ffn(prompt.x, norm_g, wg, wu, wd, i, (), pre, post, **tails[prompt])
    y_p, h_p, c_p, k_p, v_p, r_p = prompt.outputs()
    y_s, h_s, c_s, k_s, v_s, r_s = sample.outputs()
    return (y_p, y_s, h_p, c_p, k_p, v_p, r_p, h_s, c_s, k_s, v_s, r_s)
```

```python
import functools

import jax
import jax.numpy as jnp
from jax import lax
from jax.experimental import pallas as pl
from jax.experimental.pallas import tpu as pltpu

F32 = jnp.float32
BF16 = jnp.bfloat16

EPS = 1e-6
LRU_C = 8.0
CONV_W = 4
CHUNK_BITS = 6
ROPE_BASE = 10000.0
HEAD_B = 64
HEAD_PAIR = 2 * HEAD_B
SB_DEAD = -40.0
SUBLANES = 8
FF_SUB = 256
VMEM_LIMIT_BYTES = 56 * 1024 * 1024
N_NORMS = 8


def _cparams(*sem):
    return pltpu.CompilerParams(dimension_semantics=sem, vmem_limit_bytes=VMEM_LIMIT_BYTES)


def _dot(a, b):
    return jnp.dot(a, b, preferred_element_type=F32)


def _dot_nt(a, b):
    return lax.dot_general(a, b, (((1,), (1,)), ((), ())), preferred_element_type=F32)


def _dot_tn(a, b):
    return lax.dot_general(a, b, (((0,), (0,)), ((), ())), preferred_element_type=F32)


def _rms(x, g):
    return x * lax.rsqrt(jnp.mean(x * x, axis=-1, keepdims=True) + EPS) * g


def _sigmoid(x):
    return 1.0 / (1.0 + jnp.exp(-x))


def _softplus(x):
    return jnp.maximum(x, 0.0) + jnp.log(1.0 + jnp.exp(-jnp.abs(x)))


def _gelu_tanh(x):
    return 0.5 * x * (1.0 + jnp.tanh(0.7978845608028654 * (x + 0.044715 * x * x * x)))


def _row_tile(n, pref):
    t = min(n, pref)
    while n % t:
        t //= 2
    return t


def _layer_spec(arr, layer, pipeline_mode=None):
    layer = layer if isinstance(layer, tuple) else (layer,)
    rest = arr.shape[len(layer):]
    idx = layer + (0,) * len(rest)
    return pl.BlockSpec((None,) * len(layer) + rest, lambda *_: idx, pipeline_mode=pipeline_mode)


def _ffn_kernel(*refs, n_sub, n_steps, emit, pre, post, mix_post, ple_norms):
    refs = list(refs)
    x_ref, g_ref, wg_ref, wu_ref, wd_ref = refs[:5]
    n_out = 4 if emit else 1
    n_scratch = 1 if n_steps == 1 else 2
    extra = refs[5:len(refs) - n_out - n_scratch]
    o_ref = refs[len(refs) - n_out - n_scratch]
    bf_refs = refs[len(refs) - n_scratch - 3:len(refs) - n_scratch] if emit else (None, None, None)
    acc_ref = refs[len(refs) - n_scratch]
    xn_ref = refs[-1] if n_steps > 1 else None
    step = pl.program_id(1)

    def head():
        x = x_ref[...]
        if mix_post is not None:
            a_ref, b_ref, wm_ref = extra[:3]
            wa = a_ref.shape[1]
            y = _dot(a_ref[...], wm_ref[0:wa, :]) + _dot(b_ref[...], wm_ref[wa:, :])
            x = x + _rms(y, g_ref[mix_post:mix_post + 1, :])
        o_ref[...] = x
        xn = _rms(x, g_ref[pre:pre + 1, :]).astype(BF16)
        if xn_ref is not None:
            xn_ref[...] = xn
            acc_ref[...] = jnp.zeros_like(acc_ref)
        return xn

    def tail():
        x = o_ref[...] + 0.5 * _rms(acc_ref[...], g_ref[post:post + 1, :])
        if ple_norms is not None:
            p_ref, wgate_ref, win_ref = extra[-3:]
            pre2, post2 = ple_norms
            gate = _sigmoid(_dot(_rms(x, g_ref[pre2:pre2 + 1, :]).astype(BF16), wgate_ref[...]))
            e = _dot(p_ref[...].astype(BF16), win_ref[...])
            x = x + _rms(gate * e, g_ref[post2:post2 + 1, :])
        o_ref[...] = x

    def weights(w_ref, bf_ref, idx):
        w = w_ref[idx]
        if emit:
            w = w.astype(BF16)
            bf_ref[idx] = w
        return w

    if n_steps == 1:
        xn = head()
    else:
        @pl.when(step == 0)
        def _():
            head()

        xn = xn_ref[...]
    for c in range(n_sub):
        sl = pl.ds(c * FF_SUB, FF_SUB)
        h = _dot(xn, weights(wg_ref, bf_refs[0], (slice(None), sl)))
        u = _dot(xn, weights(wu_ref, bf_refs[1], (slice(None), sl)))
        a = (h * _sigmoid(h) * u).astype(BF16)
        d = _dot(a, weights(wd_ref, bf_refs[2], (sl, slice(None))))
        if c == 0 and n_steps == 1:
            acc_ref[...] = d
        else:
            acc_ref[...] += d
    if n_steps == 1:
        tail()
    else:
        @pl.when(step == n_steps - 1)
        def _():
            tail()


def _ffn(x, norm_g, wg, wu, wd, layer, which, pre, post, mix=None, ple=None, emit=False):
    n, d = x.shape
    f = wg.shape[-1]
    tm = _row_tile(n, 1024)
    tf = FF_SUB if emit else f
    n_steps = f // tf
    lead = (None,) * len(which)
    once = None if emit else pl.Buffered(1)
    rows = lambda width: pl.BlockSpec((tm, width), lambda i, s: (i, 0))
    in_specs = [
        rows(d),
        _layer_spec(norm_g, layer),
        pl.BlockSpec(lead + (d, tf), lambda i, s: which + (0, s), pipeline_mode=once),
        pl.BlockSpec(lead + (d, tf), lambda i, s: which + (0, s), pipeline_mode=once),
        pl.BlockSpec(lead + (tf, d), lambda i, s: which + (s, 0), pipeline_mode=once),
    ]
    args = [x, norm_g, wg, wu, wd]
    mix_post = ple_norms = None
    resident = pl.Buffered(1)
    if mix is not None:
        a, b, wm, j, mix_post = mix
        in_specs += [rows(a.shape[1]), rows(b.shape[1]), _layer_spec(wm, j, resident)]
        args += [a, b, wm]
    if ple is not None:
        p, wgate, win, pre2, post2 = ple
        ple_norms = (pre2, post2)
        in_specs += [
            pl.BlockSpec((None, tm, p.shape[-1]), lambda i, s: (layer, i, 0)),
            _layer_spec(wgate, layer, resident),
            _layer_spec(win, layer, resident),
        ]
        args += [p, wgate, win]
    out_specs = [rows(d)]
    out_shape = [jax.ShapeDtypeStruct((n, d), F32)]
    scratch = [pltpu.VMEM((tm, d), F32)]
    if emit:
        assert n == tm, "the weight copies are written once, by a single row tile"
        out_specs += [
            pl.BlockSpec((d, tf), lambda i, s: (0, s)),
            pl.BlockSpec((d, tf), lambda i, s: (0, s)),
            pl.BlockSpec((tf, d), lambda i, s: (s, 0)),
        ]
        out_shape += [jax.ShapeDtypeStruct((d, f), BF16)] * 2 + [jax.ShapeDtypeStruct((f, d), BF16)]
    if n_steps > 1:
        scratch.append(pltpu.VMEM((tm, d), BF16))
    out = pl.pallas_call(
        functools.partial(_ffn_kernel, n_sub=tf // FF_SUB, n_steps=n_steps, emit=emit, pre=pre, post=post,
                          mix_post=mix_post, ple_norms=ple_norms),
        grid=(n // tm, n_steps),
        in_specs=in_specs,
        out_specs=out_specs,
        out_shape=out_shape,
        scratch_shapes=scratch,
        compiler_params=_cparams("parallel", "arbitrary"),
        name="ffn" + ("_mix" if mix is not None else "") + ("_ple" if ple is not None else "")
        + ("_cast" if emit else ""),
    )(*args)
    return out if emit else out[0]


def _ab_in_kernel(*refs, w, pre, n_heads, chained):
    x_ref, g_ref, w_ref = refs[:3]
    xa_ref, ga_ref, q_ref, k_ref, v_ref, kh_ref, vh_ref = refs[5 if chained else 3:]
    hn = _rms(x_ref[...], g_ref[pre:pre + 1, :]).astype(BF16)
    for idx, o_ref in enumerate((xa_ref, ga_ref)):
        o_ref[...] = _dot(hn, w_ref[:, idx * w:(idx + 1) * w])
    hd = w // n_heads
    q_ref[...] = (_dot(hn, w_ref[:, 2 * w:3 * w]) * hd ** -0.5).astype(BF16)
    tm = x_ref.shape[0]
    for idx, o_ref, oh_ref in ((3, k_ref, kh_ref), (4, v_ref, vh_ref)):
        y = _dot(hn, w_ref[:, idx * w:(idx + 1) * w])
        o_ref[...] = y.astype(BF16)
        for h in range(n_heads):
            oh_ref[pl.ds(h, tm, stride=n_heads), :] = y[:, h * hd:(h + 1) * hd]


def _ab_in(x, norm_g, w_in, layer, j, pre, n_layers, heads_prev=None):
    n, d = x.shape
    w = w_in.shape[-1] // 5
    n_heads = w // HEAD_B
    tm = _row_tile(n, 1024)
    out = jax.ShapeDtypeStruct((n, w), F32)
    out_b = jax.ShapeDtypeStruct((n, w), BF16)
    out_h = jax.ShapeDtypeStruct((n_layers, n * n_heads, HEAD_B), F32)
    ospec = pl.BlockSpec((tm, w), lambda i: (i, 0))
    hspec = pl.BlockSpec((None, tm * n_heads, HEAD_B), lambda i: (j, i, 0))
    in_specs = [pl.BlockSpec((tm, d), lambda i: (i, 0)), _layer_spec(norm_g, layer), _layer_spec(w_in, j)]
    args = [x, norm_g, w_in]
    aliases = {}
    if heads_prev is not None:
        in_specs += [pl.BlockSpec(memory_space=pl.ANY)] * 2
        args += list(heads_prev)
        aliases = {3: 5, 4: 6}
    return pl.pallas_call(
        functools.partial(_ab_in_kernel, w=w, pre=pre, n_heads=n_heads, chained=heads_prev is not None),
        grid=(n // tm,),
        in_specs=in_specs,
        out_specs=[ospec] * 5 + [hspec] * 2,
        out_shape=[out] * 2 + [out_b] * 3 + [out_h] * 2,
        input_output_aliases=aliases,
        compiler_params=_cparams("parallel"),
        name="ab_in",
    )(*args)


def _lru_kernel(xa_ref, ga_ref, buf_ref, h0_ref, cw_ref, cb_ref, wr_ref, br_ref, wi_ref, bi_ref, lam_ref,
                oa_ref, hl_ref, nb_ref, hc_ref, xc_ref, *, t):
    l = pl.program_id(1)

    @pl.when(l == 0)
    def _():
        hc_ref[...] = h0_ref[...]
        xc_ref[...] = jnp.zeros_like(xc_ref)
        xc_ref[SUBLANES - (CONV_W - 1):, :] = buf_ref[...]

    x = xa_ref[0]
    prev = xc_ref[...]
    row8 = lax.broadcasted_iota(jnp.int32, (SUBLANES, 1), 0)
    cw = cw_ref[...]
    y = cb_ref[...] + x * cw[CONV_W - 1:CONV_W]
    for k in range(1, CONV_W):
        xs = pltpu.roll(x, k, 0)
        head = jnp.where(row8 < k, pltpu.roll(prev, k, 0), xs[:SUBLANES])
        xs = jnp.concatenate([head, xs[SUBLANES:]], axis=0)
        y = y + xs * cw[CONV_W - 1 - k:CONV_W - k]
    xc_ref[...] = x[t - SUBLANES:]

    yb = y.astype(BF16)
    rg = _sigmoid(_dot(yb, wr_ref[...]) + br_ref[...])
    ig = _sigmoid(_dot(yb, wi_ref[...]) + bi_ref[...])
    log_a = (-LRU_C) * rg * _softplus(-lam_ref[...])
    a = jnp.exp(log_a)
    b = jnp.sqrt(1.0 - jnp.exp(2.0 * log_a)) * ig * y

    row = lax.broadcasted_iota(jnp.int32, (t, 1), 0)
    d = 1
    while d < t:
        keep = row >= d
        a_sh = jnp.where(keep, pltpu.roll(a, d, 0), 1.0)
        b_sh = jnp.where(keep, pltpu.roll(b, d, 0), 0.0)
        b = a * b_sh + b
        a = a * a_sh
        d *= 2
    h = a * hc_ref[...] + b
    hc_ref[...] = h[t - 1:t]
    oa_ref[0] = (_gelu_tanh(ga_ref[0]) * h).astype(BF16)

    @pl.when(l == pl.num_programs(1) - 1)
    def _():
        hl_ref[0] = h[t - 1:t]
        nb_ref[0] = xa_ref[0, t - (CONV_W - 1):t, :]


def _lru(xa, ga, buf, h0, lp, j, js):
    bsz, length, w = xa.shape
    t = _row_tile(length, 256)
    seq = pl.BlockSpec((1, t, w), lambda b, l: (b, l, 0))
    par = lambda a: _layer_spec(a, j)
    return pl.pallas_call(
        functools.partial(_lru_kernel, t=t),
        grid=(bsz, length // t),
        in_specs=[
            seq, seq,
            pl.BlockSpec((None, None, CONV_W - 1, w), lambda b, l: (js, b, 0, 0)),
            pl.BlockSpec((None, None, 1, w), lambda b, l: (js, b, 0, 0)),
            par(lp["conv_w"]), par(lp["conv_b"]), par(lp["w_r"]), par(lp["b_r"]),
            par(lp["w_i"]), par(lp["b_i"]), par(lp["lam"]),
        ],
        out_specs=[
            seq,
            pl.BlockSpec((1, 1, w), lambda b, l: (b, 0, 0)),
            pl.BlockSpec((1, CONV_W - 1, w), lambda b, l: (b, 0, 0)),
        ],
        out_shape=[
            jax.ShapeDtypeStruct((bsz, length, w), BF16),
            jax.ShapeDtypeStruct((bsz, 1, w), F32),
            jax.ShapeDtypeStruct((bsz, CONV_W - 1, w), F32),
        ],
        scratch_shapes=[pltpu.VMEM((1, w), F32), pltpu.VMEM((SUBLANES, w), F32)],
        compiler_params=_cparams("parallel", "arbitrary"),
        name="lru",
    )(xa, ga, buf, h0, lp["conv_w"], lp["conv_b"], lp["w_r"], lp["b_r"], lp["w_i"], lp["b_i"], lp["lam"])


def _sb_step(qh, blocks, tri, cs, accs, transposed=False):
    n_pairs = len(qh)
    tq = accs[0].shape[0]
    pair = lambda p: slice(p * HEAD_PAIR, (p + 1) * HEAD_PAIR)
    if transposed:
        zs = [[_dot(qh[p], k[pair(p), :].astype(BF16)) for p in range(n_pairs)] for k, _, _ in blocks]
        low = lax.broadcasted_iota(jnp.int32, (HEAD_PAIR, 1), 0) < HEAD_B
    else:
        zs = [[_dot_nt(qh[p], k[:, pair(p)].astype(BF16)) for p in range(n_pairs)] for k, _, _ in blocks]
        low = lax.broadcasted_iota(jnp.int32, (1, HEAD_PAIR), 1) < HEAD_B
    ls = [[-_softplus(z) if m is None else jnp.where(m, -_softplus(z), 0.0) for z in zb]
          for zb, (_, _, m) in zip(zs, blocks)]
    flat = [l for lb in ls for l in lb]
    his = [l.astype(BF16) for l in flat]
    los = [(l - hi.astype(F32)).astype(BF16) for l, hi in zip(flat, his)]
    sums = _dot(jnp.concatenate(his + los, axis=0), tri)
    rows = 2 * tq
    half = len(flat) * rows
    zero = jnp.zeros((), BF16)
    cs = list(cs)
    es = [[] for _ in range(n_pairs)]
    vs = [[] for _ in range(n_pairs)]
    for b, (_, v_blk, m) in enumerate(blocks):
        for p in range(n_pairs):
            at = (b * n_pairs + p) * rows
            incl = sums[at:at + rows] + sums[half + at:half + at + rows]
            e = jnp.exp(zs[b][p] + cs[p] + incl)
            if m is not None:
                e = jnp.where(m, e, 0.0)
            e = e.astype(BF16)
            es[p] += [e[:tq], e[tq:]]
            cs[p] = cs[p] + incl[:, 0:1]
            vp = (v_blk[pair(p), :] if transposed else v_blk[:, pair(p)]).astype(BF16)
            vs[p] += [jnp.where(low, vp, zero), jnp.where(low, zero, vp)]
    accs_out = []
    for p in range(n_pairs):
        e2 = jnp.concatenate(es[p], axis=1)
        if transposed:
            accs_out.append(accs[p] + _dot_nt(e2, jnp.concatenate(vs[p], axis=1)))
        else:
            accs_out.append(accs[p] + _dot(e2, jnp.concatenate(vs[p], axis=0)))
    return tuple(cs), tuple(accs_out)


def _tri(n):
    r = lax.broadcasted_iota(jnp.int32, (n, n), 0)
    c = lax.broadcasted_iota(jnp.int32, (n, n), 1)
    return jnp.where(r >= c, 1.0, 0.0).astype(BF16)


def _sb_kernel(*refs, tq, tkp, n_past, n_pairs, with_prev):
    if n_past:
        q_ref, kn_ref, vn_ref, kp_ref, vp_ref, o_ref = refs
    else:
        q_ref, kn_ref, vn_ref, o_ref = refs
    i = pl.program_id(1)
    low = lax.broadcasted_iota(jnp.int32, (1, HEAD_PAIR), 1) < HEAD_B
    r2 = lax.broadcasted_iota(jnp.int32, (2 * tq, tq), 0)
    causal = lax.broadcasted_iota(jnp.int32, (2 * tq, tq), 1) < jnp.where(r2 >= tq, r2 - tq, r2)
    tri_n = _tri(tq)

    q = q_ref[0]
    zero = jnp.zeros((), q.dtype)
    qh = []
    for p in range(n_pairs):
        qp = q[:, p * HEAD_PAIR:(p + 1) * HEAD_PAIR]
        qh.append(jnp.concatenate([jnp.where(low, qp, zero), jnp.where(low, zero, qp)], axis=0))

    def alive(cs):
        m = functools.reduce(jnp.maximum, cs)
        return (jnp.max(m) > SB_DEAD).astype(jnp.int32)

    def sweep(first, limit, fetch, tri, flag, cs, accs, transposed=False):
        def cond(st):
            return jnp.logical_and(st[0] < limit, st[1] > 0)

        def body(st):
            n, _, cs, accs = st
            cs, accs = _sb_step(qh, [fetch(n) + (None,)], tri, cs, accs, transposed)
            return n + 1, alive(cs), cs, accs

        _, flag, cs, accs = lax.while_loop(cond, body, (jnp.int32(first), flag, cs, accs))
        return flag, cs, accs

    def fetch_new(n):
        rows = pl.ds(pl.multiple_of(jnp.maximum(i - 1 - n, 0) * tq, tq), tq)
        return kn_ref[0, rows, :], vn_ref[0, rows, :]

    def fetch_past(n):
        cols = pl.ds(pl.multiple_of((n_past - 1 - n) * tkp, tkp), tkp)
        return kp_ref[:, cols], vp_ref[:, cols]

    own = pl.ds(pl.multiple_of(i * tq, tq), tq)
    cs = tuple(jnp.zeros((2 * tq, 1), F32) for _ in range(n_pairs))
    accs = tuple(jnp.zeros((tq, HEAD_PAIR), F32) for _ in range(n_pairs))
    blocks = [(kn_ref[0, own, :], vn_ref[0, own, :], causal)]
    if with_prev:
        blocks.append(fetch_new(0) + (jnp.broadcast_to(i > 0, causal.shape),))
    cs, accs = _sb_step(qh, blocks, tri_n, cs, accs)
    flag, cs, accs = sweep(len(blocks) - 1, i, fetch_new, tri_n, alive(cs), cs, accs)
    if n_past:
        flag, cs, accs = sweep(0, n_past, fetch_past, _tri(tkp), flag, cs, accs, transposed=True)
    for p in range(n_pairs):
        o_ref[0, :, p * HEAD_PAIR:(p + 1) * HEAD_PAIR] = accs[p].astype(BF16)


def _stick_breaking(q, k_new, v_new, past=None):
    bsz, length, w = q.shape
    tq = _row_tile(length, 128)
    tkp = 128
    qspec = pl.BlockSpec((1, tq, w), lambda b, i: (b, i, 0))
    new_spec = pl.BlockSpec((1, length, w), lambda b, i: (b, 0, 0))
    in_specs = [qspec, new_spec, new_spec]
    args = [q, k_new, v_new]
    n_past = 0
    if past is not None:
        k_past, v_past, j = past
        plen = k_past.shape[3]
        n_past = plen // tkp
        past_spec = pl.BlockSpec((None, None, w, plen), lambda b, i: (j, b, 0, 0))
        in_specs += [past_spec, past_spec]
        args += [k_past, v_past]
    return pl.pallas_call(
        functools.partial(_sb_kernel, tq=tq, tkp=tkp, n_past=n_past, n_pairs=w // HEAD_PAIR,
                          with_prev=length > tq),
        grid=(bsz, length // tq),
        in_specs=in_specs,
        out_specs=qspec,
        out_shape=jax.ShapeDtypeStruct((bsz, length, w), BF16),
        compiler_params=_cparams("parallel", "arbitrary"),
        name="stick_breaking",
    )(*args)


def _ret_in_kernel(x_ref, g_ref, w_ref, cos_ref, sin_ref, qkv_ref, gate_ref, *, dk, hc, pre):
    hn = _rms(x_ref[...], g_ref[pre:pre + 1, :]).astype(BF16)
    wq = qkv_ref.shape[1]
    half = dk // 2
    cos = cos_ref[...]
    sin = sin_ref[...]
    cos_k = cos * dk ** -0.5
    sin_k = sin * dk ** -0.5
    for blk in range(w_ref.shape[1] // dk):
        lo = blk * dk
        y = _dot(hn, w_ref[:, lo:lo + dk])
        if blk < 2 * hc:
            c, s = (cos, sin) if blk < hc else (cos_k, sin_k)
            x1 = y[:, :half]
            x2 = y[:, half:]
            qkv_ref[:, lo:lo + half] = (x1 * c - x2 * s).astype(BF16)
            qkv_ref[:, lo + half:lo + dk] = (x2 * c + x1 * s).astype(BF16)
        elif lo < wq:
            qkv_ref[:, lo:lo + dk] = y.astype(BF16)
        else:
            gate_ref[:, lo - wq:lo - wq + dk] = y.astype(BF16)


def _ret_in(x, norm_g, w_in, cos, sin, hc, dk, wv, layer, j, pre):
    n, d = x.shape
    cols = w_in.shape[-1]
    tm = _row_tile(min(n, cos.shape[0]), 512)
    nmod = cos.shape[0] // tm
    tab = pl.BlockSpec((tm, dk // 2), lambda i: (i % nmod, 0))
    return pl.pallas_call(
        functools.partial(_ret_in_kernel, dk=dk, hc=hc, pre=pre),
        grid=(n // tm,),
        in_specs=[
            pl.BlockSpec((tm, d), lambda i: (i, 0)),
            _layer_spec(norm_g, layer),
            _layer_spec(w_in, j, pl.Buffered(1)),
            tab, tab,
        ],
        out_specs=[
            pl.BlockSpec((tm, cols - wv), lambda i: (i, 0)),
            pl.BlockSpec((tm, wv), lambda i: (i, 0)),
        ],
        out_shape=[jax.ShapeDtypeStruct((n, cols - wv), BF16), jax.ShapeDtypeStruct((n, wv), BF16)],
        compiler_params=_cparams("parallel"),
        name="ret_in",
    )(x, norm_g, w_in, cos, sin)


def _ret_kernel(*refs, t, hc, dk, dv, has_state, chained):
    lg_ref, q_ref, k_ref, v_ref = refs[:4]
    r0_ref = refs[4] if has_state else None
    o_ref, rl_ref, r_ref = refs[4 + has_state + chained:]
    l = pl.program_id(1)

    @pl.when(l == 0)
    def _():
        r_ref[...] = r0_ref[...] if has_state else jnp.zeros_like(r_ref)

    ri = lax.broadcasted_iota(jnp.int32, (t, t), 0)
    ci = lax.broadcasted_iota(jnp.int32, (t, t), 1)
    dist = jnp.abs(ri - ci).astype(F32)
    visible = (ci >> CHUNK_BITS) <= (ri >> CHUNK_BITS)
    pos = lax.broadcasted_iota(jnp.int32, (t, 1), 0).astype(F32)
    heads = range(hc)
    lg = [lg_ref[h] for h in heads]
    q = [q_ref[0, :, h * dk:(h + 1) * dk] for h in heads]
    k = [k_ref[0, :, h * dk:(h + 1) * dk] for h in heads]
    v = [v_ref[0, :, h * dv:(h + 1) * dv] for h in heads]
    r = [r_ref[h] for h in heads]
    s = [_dot_nt(q[h], k[h]) for h in heads]
    cross = [_dot(q[h], r[h].astype(BF16)) for h in heads]
    kz = [(k[h].astype(F32) * jnp.exp(lg[h] * (t - 1.0 - pos))).astype(BF16) for h in heads]
    upd = [_dot_tn(kz[h], v[h]) for h in heads]
    for h in heads:
        sm = jnp.where(visible, s[h] * jnp.exp(lg[h] * dist), 0.0).astype(BF16)
        o = _dot(sm, v[h]) + cross[h] * jnp.exp(lg[h] * (pos + 1.0))
        o_ref[0, :, h * dv:(h + 1) * dv] = o.astype(BF16)
        r_ref[h] = jnp.exp(jnp.full((1, 1), t, F32) * lg[h]) * r[h] + upd[h]

    @pl.when(l == pl.num_programs(1) - 1)
    def _():
        rl_ref[...] = r_ref[...]


def _retention(qkv, state, log_g, bsz, length, hc, dk, dv, j, n_layers, finals_prev=None):
    t = _row_tile(length, 256)
    wq, wv = hc * dk, hc * dv
    in_specs = [
        pl.BlockSpec(memory_space=pltpu.SMEM),
        pl.BlockSpec((1, t, wq), lambda b, l: (b, l, 0)),
        pl.BlockSpec((1, t, wq), lambda b, l: (b, l, 1)),
        pl.BlockSpec((1, t, wv), lambda b, l: (b, l, 2 * wq // wv)),
    ]
    args = [log_g, qkv, qkv, qkv]
    state_spec = pl.BlockSpec((None, None, hc, dk, dv), lambda b, l: (j, b, 0, 0, 0))
    if state is not None:
        in_specs.append(state_spec)
        args.append(state)
    aliases = {}
    if finals_prev is not None:
        aliases = {len(args): 1}
        in_specs.append(pl.BlockSpec(memory_space=pl.ANY))
        args.append(finals_prev)
    return pl.pallas_call(
        functools.partial(_ret_kernel, t=t, hc=hc, dk=dk, dv=dv, has_state=state is not None,
                          chained=finals_prev is not None),
        grid=(bsz, length // t),
        in_specs=in_specs,
        out_specs=[pl.BlockSpec((1, t, wv), lambda b, l: (b, l, 0)), state_spec],
        out_shape=[
            jax.ShapeDtypeStruct((bsz, length, wv), BF16),
            jax.ShapeDtypeStruct((n_layers, bsz, hc, dk, dv), F32),
        ],
        input_output_aliases=aliases,
        scratch_shapes=[pltpu.VMEM((hc, dk, dv), F32)],
        compiler_params=_cparams("parallel", "arbitrary"),
        name="retention",
    )(*args)


def _ret_out_kernel(x_ref, o_ref_in, gate_ref, g_ref, w_ref, out_ref, *, hc, dv, post):
    y = None
    for h in range(hc):
        sl = slice(h * dv, (h + 1) * dv)
        o = o_ref_in[:, sl].astype(F32)
        mu = jnp.mean(o, axis=-1, keepdims=True)
        oc = o - mu
        on = oc * lax.rsqrt(jnp.mean(oc * oc, axis=-1, keepdims=True) + EPS)
        gt = gate_ref[:, sl].astype(F32)
        d = _dot((gt * _sigmoid(gt) * on).astype(BF16), w_ref[sl, :])
        y = d if y is None else y + d
    out_ref[...] = x_ref[...] + _rms(y, g_ref[post:post + 1, :])


def _ret_out(x, o, gate, norm_g, w_out, hc, dv, layer, j, post):
    n, d = x.shape
    wv = hc * dv
    tm = _row_tile(n, 1024)
    return pl.pallas_call(
        functools.partial(_ret_out_kernel, hc=hc, dv=dv, post=post),
        grid=(n // tm,),
        in_specs=[
            pl.BlockSpec((tm, d), lambda i: (i, 0)),
            pl.BlockSpec((tm, wv), lambda i: (i, 0)),
            pl.BlockSpec((tm, wv), lambda i: (i, 0)),
            _layer_spec(norm_g, layer),
            _layer_spec(w_out, j),
        ],
        out_specs=pl.BlockSpec((tm, d), lambda i: (i, 0)),
        out_shape=jax.ShapeDtypeStruct((n, d), F32),
        compiler_params=_cparams("parallel"),
        name="ret_out",
    )(x, o, gate, norm_g, w_out)


def _block_diag(w):
    n, h, bi, bj = w.shape
    eye = jnp.eye(h, dtype=w.dtype)
    return (eye[None, :, None, :, None] * w[:, :, :, None, :]).reshape(n, h * bi, h * bj)


class _Group:
    def __init__(self, x, p, conv_state, lru_state, past_k, past_v, ret_state, pos0, wts):
        self.bsz, self.length, d = x.shape
        bsz, length = self.bsz, self.length
        self.n = n = bsz * length
        depth = p.shape[0]
        self.x = x.reshape(n, d)
        self.p = p.reshape(depth, n, -1)
        dk = wts["dk"]
        self.w = w = wts["lru"]["lam"].shape[-1]
        pos = (pos0 + jnp.arange(length)).astype(F32)
        half = dk // 2
        freq = ROPE_BASE ** (-jnp.arange(half, dtype=F32) / half)
        ang = pos[:, None] * freq[None, :]
        reps = max(1, min(n, 1024) // length)
        self.cos = jnp.tile(jnp.cos(ang), (reps, 1))
        self.sin = jnp.tile(jnp.sin(ang), (reps, 1))
        self.fresh = conv_state is None
        if self.fresh:
            conv_state = jnp.zeros((1, bsz, CONV_W - 1, w), F32)
            lru_state = jnp.zeros((1, bsz, w), F32)
        self.conv_state = conv_state
        self.lru_state = lru_state.reshape(-1, bsz, 1, w)
        self.past = None
        if not self.fresh:
            plen = past_k.shape[2]
            self.past = (past_k.transpose(0, 1, 3, 4, 2).reshape(-1, bsz, w, plen),
                         past_v.transpose(0, 1, 3, 4, 2).reshape(-1, bsz, w, plen))
        self.ret_state = ret_state
        self.n_even = (depth + 1) // 2
        self.n_odd = depth // 2
        self.new_h, self.new_conv, self.new_ret, self.kv_heads = [], [], None, None

    def mixer(self, i, wts):
        bsz, length, n, w = self.bsz, self.length, self.n, self.w
        norm_g = wts["norm_g"]
        hc, dk, dv = wts["hc"], wts["dk"], wts["dv"]
        seq = lambda a: a.reshape(bsz, length, -1)
        j = i // 2
        if i % 2 == 0:
            js = 0 if self.fresh else j
            xa, ga, q, k, v, *self.kv_heads = _ab_in(self.x, norm_g, wts["ab_w_in"], i, j, 2, self.n_even,
                                                     self.kv_heads)
            oa, h_last, nbuf = _lru(seq(xa), seq(ga), self.conv_state, self.lru_state, wts["lru"], j, js)
            ob = _stick_breaking(seq(q), seq(k), seq(v), None if self.fresh else self.past + (j,))
            self.new_h.append(h_last.reshape(bsz, w))
            self.new_conv.append(nbuf)
            return oa.reshape(n, w), ob.reshape(n, w), wts["ab_w_out"], j, 3
        qkv, gate = _ret_in(self.x, norm_g, wts["ret_w_in"], self.cos, self.sin, hc, dk, hc * dv, i, j, 2)
        o, self.new_ret = _retention(seq(qkv), self.ret_state, wts["log_g"], bsz, length, hc, dk, dv,
                                     j, self.n_odd, self.new_ret)
        self.x = _ret_out(self.x, o.reshape(n, hc * dv), gate, norm_g, wts["ret_w_out"], hc, dv, i, j, 3)
        return None

    def outputs(self):
        d = self.x.shape[1]
        new_k, new_v = (a.reshape(self.n_even, self.bsz, self.length, self.w // HEAD_B, HEAD_B)
                        for a in self.kv_heads)
        return (self.x.reshape(self.bsz, self.length, d), jnp.stack(self.new_h), jnp.stack(self.new_conv),
                new_k, new_v, self.new_ret)


def kernel(x_prompt, x_sample, p_prompt, p_sample, state_lru_h, state_conv, cache_sb_k, cache_sb_v, state_ret,
           norm_g, ffn_w_gate, ffn_w_up, ffn_w_down, ple_w_in, ple_w_gate,
           ab_w_in, ab_w_out, lru_conv_w, lru_conv_b, lru_w_r, lru_b_r, lru_w_i, lru_b_i, lru_lambda,
           ret_w_in, ret_w_out):
    assert norm_g.shape[1] == N_NORMS
    hc, dk, dv = state_ret.shape[2:]
    row = lambda a: a[:, None, :]
    wts = dict(
        hc=hc, dk=dk, dv=dv,
        norm_g=norm_g,
        ple_w_in=ple_w_in.astype(BF16), ple_w_gate=ple_w_gate.astype(BF16),
        ab_w_in=ab_w_in.astype(BF16), ab_w_out=ab_w_out.astype(BF16),
        lru=dict(
            conv_w=lru_conv_w, conv_b=row(lru_conv_b),
            w_r=_block_diag(lru_w_r).astype(BF16), b_r=row(lru_b_r),
            w_i=_block_diag(lru_w_i).astype(BF16), b_i=row(lru_b_i),
            lam=row(lru_lambda),
        ),
        ret_w_in=ret_w_in.astype(BF16), ret_w_out=ret_w_out.astype(BF16),
        log_g=jnp.log(1.0 - 2.0 ** (-5.0 - jnp.arange(hc, dtype=F32))),
    )
    prompt = _Group(x_prompt, p_prompt, None, None, None, None, None, 0, wts)
    sample = _Group(x_sample, p_sample, state_conv, state_lru_h, cache_sb_k, cache_sb_v, state_ret,
                    cache_sb_k.shape[2], wts)
    for i in range(norm_g.shape[0]):
        for which, (pre, post) in enumerate(((0, 1), (4, 5))):
            tails = {}
            for g in (sample, prompt):
                if which == 1:
                    tails[g] = dict(mix=g.mixer(i, wts), ple=(g.p, wts["ple_w_gate"], wts["ple_w_in"], 6, 7))
                else:
                    tails[g] = {}
            sample.x, wg, wu, wd = _ffn(sample.x, norm_g, ffn_w_gate, ffn_w_up, ffn_w_down, i, (i, which),
                                        pre, post, emit=True, **tails[sample])
            prompt.x = _ffn(prompt.x, norm_g, wg, wu, wd, i, (), pre, post, **tails[prompt])
    y_p, h_p, c_p, k_p, v_p, r_p = prompt.outputs()
    y_s, h_s, c_s, k_s, v_s, r_s = sample.outputs()
    return (y_p, y_s, h_p, c_p, k_p, v_p, r_p, h_s, c_s, k_s, v_s, r_s)
```
